```python
import math
import jax, jax.numpy as jnp
from jax import lax
import numpy as np

D_MODEL = 1024
BATCH = 8
SEQ = 4096
DEPTH = 2

N_MIXERS = 2
D_FF = 256 * (-(-(8 * D_MODEL) // (3 * 256)))
ROPE_THETA = 10000.0
EPS = 1e-6
NEG_INF = -1e30
A_HEAD_DIM = 64
A_HEADS = D_MODEL // (2 * A_HEAD_DIM)
A_Q_BLOCK = 128
B_HEAD_DIM = 64
B_HEADS = D_MODEL // B_HEAD_DIM
B_GROUPS = ((128, 1), (512, 4), (2048, 16))
B_BLOCK = 64
N_A_LAYERS = (DEPTH + 1) // 2
N_B_LAYERS = DEPTH // 2

kernel_name = 'hybrid_diffattn_dilated_macaron_encoder'


def rms_norm(x, g):
    xf = x.astype(jnp.float32)
    y = xf * lax.rsqrt(jnp.mean(xf * xf, axis=-1, keepdims=True) + EPS)
    return (y * g.astype(jnp.float32)).astype(x.dtype)


def rope_tables(seq, dim):
    inv = ROPE_THETA ** (-jnp.arange(0, dim, 2, dtype=jnp.float32) / dim)
    ang = jnp.arange(seq, dtype=jnp.float32)[:, None] * inv[None, :]
    return jnp.cos(ang), jnp.sin(ang)


def apply_rope(x, cos, sin):
    shape = (cos.shape[0],) + (1,) * (x.ndim - 3) + (cos.shape[1],)
    c, s = cos.reshape(shape), sin.reshape(shape)
    xf = x.astype(jnp.float32)
    x1, x2 = jnp.split(xf, 2, axis=-1)
    return jnp.concatenate([x1 * c - x2 * s, x2 * c + x1 * s], axis=-1).astype(x.dtype)


def swiglu(h, w_gate, w_up, w_down):
    return (jax.nn.silu(h @ w_gate) * (h @ w_up)) @ w_down


def diff_attention(h, w_qkv, w_o, lam, subln, lambda_init, cos, sin):
    B_, S_, _ = h.shape
    q, k, v = jnp.split(h @ w_qkv, 3, axis=-1)
    q = q.reshape(B_, S_, A_HEADS, 2, A_HEAD_DIM)
    k = k.reshape(B_, S_, A_HEADS, 2, A_HEAD_DIM)
    v = v.reshape(B_, S_, A_HEADS, 2 * A_HEAD_DIM)
    q = apply_rope(q, cos, sin) * (A_HEAD_DIM ** -0.5)
    k = apply_rope(k, cos, sin)
    lamf = lam.astype(jnp.float32)
    lam_full = jnp.exp(jnp.sum(lamf[0] * lamf[1])) - jnp.exp(jnp.sum(lamf[2] * lamf[3])) + lambda_init
    nq = S_ // A_Q_BLOCK
    qb = q.reshape(B_, nq, A_Q_BLOCK, A_HEADS, 2, A_HEAD_DIM).transpose(1, 0, 2, 3, 4, 5)

    def block(qblk):
        s = jnp.einsum('bqhcd,bkhcd->bhcqk', qblk, k, preferred_element_type=jnp.float32)
        p = jax.nn.softmax(s, axis=-1)
        a = p[:, :, 0] - lam_full * p[:, :, 1]
        return jnp.einsum('bhqk,bkhe->bqhe', a.astype(v.dtype), v)

    o = lax.map(block, qb)
    o = o.transpose(1, 0, 2, 3, 4).reshape(B_, S_, A_HEADS, 2 * A_HEAD_DIM)
    o = rms_norm(o, subln) * (1.0 - lambda_init)
    return o.reshape(B_, S_, D_MODEL) @ w_o


def dilated_group_attention(q, k, v, dilation, half):
    B_, S_, H_, hd = q.shape
    L = S_ // dilation
    nb = -(-L // B_BLOCK)
    Lp = nb * B_BLOCK

    def to_strided(t):
        t = t.reshape(B_, L, dilation, H_, t.shape[-1]).transpose(0, 2, 3, 1, 4)
        return jnp.pad(t, ((0, 0), (0, 0), (0, 0), (0, Lp - L), (0, 0)))

    def band(t):
        tp = jnp.pad(t, ((0, 0), (0, 0), (0, 0), (B_BLOCK, B_BLOCK), (0, 0)))
        tb = tp.reshape(B_, dilation, H_, nb + 2, B_BLOCK, t.shape[-1])
        return jnp.concatenate([tb[:, :, :, :-2], tb[:, :, :, 1:-1], tb[:, :, :, 2:]], axis=4)

    qb = to_strided(q).reshape(B_, dilation, H_, nb, B_BLOCK, hd)
    kb = band(to_strided(k))
    vb = band(to_strided(v))
    s = jnp.einsum('bphnqe,bphnke->bphnqk', qb, kb, preferred_element_type=jnp.float32)
    qi = jnp.arange(Lp).reshape(nb, B_BLOCK, 1)
    kj = (jnp.arange(nb)[:, None] * B_BLOCK - B_BLOCK + jnp.arange(3 * B_BLOCK)[None, :]).reshape(nb, 1, 3 * B_BLOCK)
    valid = (jnp.abs(qi - kj) <= half) & (kj >= 0) & (kj < L)
    s = jnp.where(valid, s, NEG_INF)
    m = jnp.max(s, axis=-1, keepdims=True)
    p = jnp.exp(s - m)
    den = jnp.sum(p, axis=-1, keepdims=True)
    o = jnp.einsum('bphnqk,bphnke->bphnqe', (p / den).astype(v.dtype), vb)
    lse = m + jnp.log(den)

    def from_strided(t):
        t = t[:, :, :, :L]
        return t.transpose(0, 3, 1, 2, 4).reshape(B_, S_, H_, t.shape[-1])

    o = from_strided(o.reshape(B_, dilation, H_, Lp, hd))
    lse = from_strided(lse.reshape(B_, dilation, H_, Lp, 1))[..., 0]
    return o, lse


def dilated_mixture_attention(h, w_in, w_o, cos, sin):
    B_, S_, _ = h.shape
    n_groups = len(B_GROUPS)
    parts = jnp.split(h @ w_in, 2 * n_groups + 1, axis=-1)
    v = parts[-1].reshape(B_, S_, B_HEADS, B_HEAD_DIM)
    outs, lses = [], []
    for g, (window, dilation) in enumerate(B_GROUPS):
        q = apply_rope(parts[2 * g].reshape(B_, S_, B_HEADS, B_HEAD_DIM), cos, sin) * (B_HEAD_DIM ** -0.5)
        k = apply_rope(parts[2 * g + 1].reshape(B_, S_, B_HEADS, B_HEAD_DIM), cos, sin)
        o, lse = dilated_group_attention(q, k, v, dilation, window // (2 * dilation))
        outs.append(o)
        lses.append(lse)
    alpha = jax.nn.softmax(jnp.stack(lses, axis=0), axis=0)
    o = jnp.sum(jnp.stack(outs, axis=0).astype(jnp.float32) * alpha[..., None], axis=0).astype(h.dtype)
    return o.reshape(B_, S_, D_MODEL) @ w_o


def setup_inputs(seed: int = 0) -> dict:
    key = jax.random.key(seed)
    ks = jax.random.split(key, 20)
    n_in = 2 * len(B_GROUPS) + 1
    f32 = jnp.float32

    def w(k, shape, fan_in):
        return jax.random.normal(k, shape, f32) * (fan_in ** -0.5)

    def gain(k, shape):
        return 1.0 + 0.01 * jax.random.normal(k, shape, f32)

    return {
        'x': jax.random.normal(ks[0], (BATCH, SEQ, D_MODEL), f32),
        'ln_ffn1': gain(ks[1], (DEPTH, D_MODEL)),
        'w1_gate': w(ks[2], (DEPTH, D_MODEL, D_FF), D_MODEL),
        'w1_up': w(ks[3], (DEPTH, D_MODEL, D_FF), D_MODEL),
        'w1_down': w(ks[4], (DEPTH, D_FF, D_MODEL), D_FF),
        'ln_mix': gain(ks[5], (DEPTH, D_MODEL)),
        'a_w_qkv': w(ks[6], (N_A_LAYERS, D_MODEL, 3 * D_MODEL), D_MODEL),
        'a_w_o': w(ks[7], (N_A_LAYERS, D_MODEL, D_MODEL), D_MODEL),
        'a_lambda': 0.1 * jax.random.normal(ks[8], (N_A_LAYERS, 4, A_HEAD_DIM), f32),
        'a_subln': gain(ks[9], (N_A_LAYERS, 2 * A_HEAD_DIM)),
        'b_w_in': w(ks[10], (N_B_LAYERS, D_MODEL, n_in * D_MODEL), D_MODEL),
        'b_w_o': w(ks[11], (N_B_LAYERS, D_MODEL, D_MODEL), D_MODEL),
        'ln_ffn2': gain(ks[12], (DEPTH, D_MODEL)),
        'w2_gate': w(ks[13], (DEPTH, D_MODEL, D_FF), D_MODEL),
        'w2_up': w(ks[14], (DEPTH, D_MODEL, D_FF), D_MODEL),
        'w2_down': w(ks[15], (DEPTH, D_FF, D_MODEL), D_FF),
        'ln_final': gain(ks[16], (D_MODEL,)),
    }


def reference(x, ln_ffn1, w1_gate, w1_up, w1_down, ln_mix, a_w_qkv, a_w_o, a_lambda, a_subln,
              b_w_in, b_w_o, ln_ffn2, w2_gate, w2_up, w2_down, ln_final):
    cos, sin = rope_tables(x.shape[1], A_HEAD_DIM)
    for i in range(DEPTH):
        x = x + 0.5 * swiglu(rms_norm(x, ln_ffn1[i]), w1_gate[i], w1_up[i], w1_down[i])
        h = rms_norm(x, ln_mix[i])
        j = i // N_MIXERS
        if i % N_MIXERS == 0:
            lambda_init = 0.8 - 0.6 * math.exp(-0.3 * i)
            x = x + diff_attention(h, a_w_qkv[j], a_w_o[j], a_lambda[j], a_subln[j], lambda_init, cos, sin)
        else:
            x = x + dilated_mixture_attention(h, b_w_in[j], b_w_o[j], cos, sin)
        x = x + 0.5 * swiglu(rms_norm(x, ln_ffn2[i]), w2_gate[i], w2_up[i], w2_down[i])
    return rms_norm(x, ln_final)
```

```python
import functools
import math

import jax
import jax.numpy as jnp
from jax import lax
from jax.experimental import pallas as pl
from jax.experimental.pallas import tpu as pltpu

EPS = 1e-6
ROPE_THETA = 10000.0
NEG_INF = -1e30
HEAD_DIM = 64
ROT_HALF = HEAD_DIM // 2
LANES = 128
B_GROUPS = ((128, 1), (512, 4), (2048, 16))
B_BLOCK = 64

ROW_TILE = 512
FF_CHUNK = 256
A_Q_TILE = 256
B_Q_TILE = 128
VMEM_LIMIT = 56 * 1024 * 1024

F32 = jnp.float32
BF16 = jnp.bfloat16


def _resident(shape):
    return pl.BlockSpec(shape, lambda *_: (0,) * len(shape), pipeline_mode=pl.Buffered(1))


def _rms_scale(x):
    return x * lax.rsqrt(jnp.mean(x * x, axis=-1, keepdims=True) + EPS)


def _ffn_kernel(x_ref, g_ref, wg_ref, wu_ref, wd_ref, *rest, n_chunks, final):
    if final:
        gf_ref, o_ref = rest
    else:
        (o_ref,) = rest
    x = x_ref[...]
    h = (_rms_scale(x) * g_ref[...]).astype(BF16)
    acc = jnp.zeros(x.shape, F32)
    for c in range(n_chunks):
        cols = slice(c * FF_CHUNK, (c + 1) * FF_CHUNK)
        gate = jnp.dot(h, wg_ref[:, cols], preferred_element_type=F32)
        up = jnp.dot(h, wu_ref[:, cols], preferred_element_type=F32)
        a = (gate * jax.nn.sigmoid(gate) * up).astype(BF16)
        acc = acc + jnp.dot(a, wd_ref[cols, :], preferred_element_type=F32)
    y = x + 0.5 * acc
    if final:
        y = _rms_scale(y) * gf_ref[...]
    o_ref[...] = y


def _ffn(x, g, wg, wu, wd, final_g=None):
    n, d = x.shape
    ff = wg.shape[1]
    assert n % ROW_TILE == 0 and ff % FF_CHUNK == 0
    final = final_g is not None
    row = pl.BlockSpec((ROW_TILE, d), lambda i: (i, 0))
    in_specs = [row, _resident((1, d)), _resident((d, ff)), _resident((d, ff)), _resident((ff, d))]
    args = [x, g.reshape(1, d), wg.astype(BF16), wu.astype(BF16), wd.astype(BF16)]
    if final:
        in_specs.append(_resident((1, d)))
        args.append(final_g.reshape(1, d))
    return pl.pallas_call(
        functools.partial(_ffn_kernel, n_chunks=ff // FF_CHUNK, final=final),
        grid=(n // ROW_TILE,),
        in_specs=in_specs,
        out_specs=row,
        out_shape=jax.ShapeDtypeStruct((n, d), F32),
        compiler_params=pltpu.CompilerParams(
            dimension_semantics=("parallel",), vmem_limit_bytes=VMEM_LIMIT),
        name="ffn_final" if final else "ffn",
    )(*args)


def _proj_kernel(x_ref, g_ref, w_ref, cq_ref, sq_ref, ck_ref, sk_ref, *out_refs, kinds):
    x = x_ref[...]
    d = x.shape[1]
    h = (_rms_scale(x) * g_ref[...]).astype(BF16)
    for c, kind in enumerate(kinds):
        y = jnp.dot(h, w_ref[:, c * d:(c + 1) * d], preferred_element_type=F32)
        if kind == "v":
            out_refs[c][...] = y.astype(BF16)
            continue
        cos = (cq_ref if kind == "q" else ck_ref)[...]
        sin = (sq_ref if kind == "q" else sk_ref)[...]
        for j in range(d // LANES):
            s = y[:, j * LANES:(j + 1) * LANES]
            r = s * cos + pltpu.roll(s, LANES // 2, 1) * sin
            out_refs[c][:, j * LANES:(j + 1) * LANES] = r.astype(BF16)


def _proj(x, g, w, tables, kinds, seq, name):
    n, d = x.shape
    assert n % ROW_TILE == 0 and seq % ROW_TILE == 0 and w.shape[1] == d * len(kinds)
    pos_blocks = seq // ROW_TILE
    row = pl.BlockSpec((ROW_TILE, d), lambda i: (i, 0))
    tab = pl.BlockSpec((ROW_TILE, LANES), lambda i: (i % pos_blocks, 0))
    return pl.pallas_call(
        functools.partial(_proj_kernel, kinds=kinds),
        grid=(n // ROW_TILE,),
        in_specs=[row, _resident((1, d)), _resident(w.shape), tab, tab, tab, tab],
        out_specs=[row] * len(kinds),
        out_shape=[jax.ShapeDtypeStruct((n, d), BF16)] * len(kinds),
        compiler_params=pltpu.CompilerParams(
            dimension_semantics=("parallel",), vmem_limit_bytes=VMEM_LIMIT),
        name=name,
    )(x, g.reshape(1, d), w, *tables)


def _rope_tables(seq):
    inv = ROPE_THETA ** (-jnp.arange(0, HEAD_DIM, 2, dtype=F32) / HEAD_DIM)
    ang = jnp.arange(seq, dtype=F32)[:, None] * inv[None, :]
    cos, sin = jnp.cos(ang), jnp.sin(ang)
    cos_t = jnp.concatenate([cos, cos, cos, cos], axis=1)
    sin_t = jnp.concatenate([-sin, -sin, sin, sin], axis=1)
    scale = HEAD_DIM ** -0.5
    return cos_t * scale, sin_t * scale, cos_t, sin_t


def _pair_halves_layout(w_chunk):
    din, dout = w_chunk.shape
    w5 = w_chunk.reshape(din, dout // LANES, 2, 2, ROT_HALF)
    return w5.transpose(0, 1, 3, 2, 4).reshape(din, dout)


def _stack_pair(q):
    lane = lax.broadcasted_iota(jnp.int32, q.shape, 1)
    first = (lane % HEAD_DIM) < ROT_HALF
    zero = jnp.zeros_like(q)
    return jnp.concatenate([jnp.where(first, q, zero), jnp.where(first, zero, q)], axis=0)


def _attn_a_kernel(lam_ref, subln_ref, q_ref, k_ref, v_ref, o_ref, *, lambda_init):
    tq = q_ref.shape[1]
    q2 = _stack_pair(q_ref[0])
    s = lax.dot_general(q2, k_ref[0], (((1,), (1,)), ((), ())),
                        preferred_element_type=F32)
    m = jnp.max(s, axis=-1, keepdims=True)
    p = jnp.exp(s - m)
    l = jnp.sum(p, axis=-1, keepdims=True)
    pv = jnp.dot(p.astype(BF16), v_ref[0], preferred_element_type=F32) / l
    lam = lam_ref[...]
    lam_full = (jnp.exp(jnp.sum(lam[0:1] * lam[1:2], axis=-1, keepdims=True))
                - jnp.exp(jnp.sum(lam[2:3] * lam[3:4], axis=-1, keepdims=True)) + lambda_init)
    o = pv[:tq] - lam_full * pv[tq:]
    o = _rms_scale(o) * subln_ref[...] * (1.0 - lambda_init)
    o_ref[0] = o.astype(BF16)


def _attn_a(q, k, v, lam, subln, lambda_init):
    b, s, d = q.shape
    heads = d // LANES
    tq = min(A_Q_TILE, s)
    assert s % tq == 0
    qo = pl.BlockSpec((1, tq, LANES), lambda bi, hi, qi: (bi, qi, hi))
    kv = pl.BlockSpec((1, s, LANES), lambda bi, hi, qi: (bi, 0, hi))
    return pl.pallas_call(
        functools.partial(_attn_a_kernel, lambda_init=lambda_init),
        grid=(b, heads, s // tq),
        in_specs=[_resident(lam.shape), _resident((1, LANES)), qo, kv, kv],
        out_specs=qo,
        out_shape=jax.ShapeDtypeStruct((b, s, d), BF16),
        compiler_params=pltpu.CompilerParams(
            dimension_semantics=("parallel", "parallel", "arbitrary"),
            vmem_limit_bytes=VMEM_LIMIT),
        name="diff_attn",
    )(lam, subln.reshape(1, LANES), q, k, v)


def _out_kernel(x_ref, o_ref, w_ref, y_ref):
    y_ref[...] = x_ref[...] + jnp.dot(o_ref[...], w_ref[...], preferred_element_type=F32)


def _out_proj(x, o, w):
    n, d = x.shape
    row = pl.BlockSpec((ROW_TILE, d), lambda i: (i, 0))
    return pl.pallas_call(
        _out_kernel,
        grid=(n // ROW_TILE,),
        in_specs=[row, row, _resident(w.shape)],
        out_specs=row,
        out_shape=jax.ShapeDtypeStruct((n, d), F32),
        compiler_params=pltpu.CompilerParams(
            dimension_semantics=("parallel",), vmem_limit_bytes=VMEM_LIMIT),
        name="out_proj",
    )(x, o, w)


def _attn_b_kernel(q_ref, k_ref, v_ref, o_ref, lse_ref, *, seq_len, half, blocks_per_phase):
    cw = q_ref.shape[2]
    tq = min(B_Q_TILE, seq_len)
    win = tq + 2 * half
    n_pairs = cw // LANES
    j = pl.program_id(1)
    first_pair = (j % blocks_per_phase) * n_pairs

    @pl.when(j % blocks_per_phase == 0)
    def _():
        lse_ref[...] = jnp.zeros(lse_ref.shape, F32)

    lane = lax.broadcasted_iota(jnp.int32, (tq, LANES), 1)
    row = lax.broadcasted_iota(jnp.int32, (2 * tq, win), 0) % tq
    col = lax.broadcasted_iota(jnp.int32, (2 * tq, win), 1)

    for pr in range(n_pairs):
        cols = slice(pr * LANES, (pr + 1) * LANES)
        lane_a = 2 * (first_pair + pr)

        def tile(t, carry, cols=cols, lane_a=lane_a):
            i0 = pl.multiple_of(t * tq, tq)
            w0 = pl.multiple_of(jnp.clip(i0 - half, 0, seq_len - win), half)
            q2 = _stack_pair(q_ref[0, pl.ds(i0, tq), cols])
            kw = k_ref[0, pl.ds(w0, win), cols]
            vw = v_ref[0, pl.ds(w0, win), cols]
            s = lax.dot_general(q2, kw, (((1,), (1,)), ((), ())),
                                preferred_element_type=F32)
            valid = jnp.abs((row + i0) - (col + w0)) <= half
            s = jnp.where(valid, s, NEG_INF)
            m = jnp.max(s, axis=-1, keepdims=True)
            p = jnp.exp(s - m)
            l = jnp.sum(p, axis=-1, keepdims=True)
            pv = jnp.dot(p.astype(BF16), vw, preferred_element_type=F32) / l
            o_ref[0, pl.ds(i0, tq), cols] = jnp.where(
                lane < HEAD_DIM, pv[:tq], pv[tq:]).astype(BF16)
            lse = m + jnp.log(l)
            cur = lse_ref[0, pl.ds(i0, tq), :]
            cur = jnp.where(lane == lane_a, lse[:tq], jnp.where(lane == lane_a + 1, lse[tq:], cur))
            lse_ref[0, pl.ds(i0, tq), :] = cur
            return carry

        lax.fori_loop(0, seq_len // tq, tile, 0)


def _attn_b_group(q, k, v, dilation, half):
    b, s, d = q.shape
    sl = s // dilation
    tq = min(B_Q_TILE, sl)
    assert s % dilation == 0 and sl % tq == 0 and sl >= tq + 2 * half
    cw = max(LANES, min(d, (1 << 20) // sl))
    assert d % cw == 0
    blocks_per_phase = d // cw
    view = lambda t: t.reshape(b, sl, dilation * d)
    blk = pl.BlockSpec((1, sl, cw), lambda bi, j: (bi, 0, j))
    lse_blk = pl.BlockSpec((1, sl, LANES), lambda bi, j: (bi, 0, j // blocks_per_phase))
    o, lse = pl.pallas_call(
        functools.partial(_attn_b_kernel, seq_len=sl, half=half, blocks_per_phase=blocks_per_phase),
        grid=(b, dilation * blocks_per_phase),
        in_specs=[blk, blk, blk],
        out_specs=[blk, lse_blk],
        out_shape=[jax.ShapeDtypeStruct((b, sl, dilation * d), BF16),
                   jax.ShapeDtypeStruct((b, sl, dilation * LANES), F32)],
        compiler_params=pltpu.CompilerParams(
            dimension_semantics=("parallel", "arbitrary"), vmem_limit_bytes=VMEM_LIMIT),
        name=f"dilated_attn_d{dilation}",
    )(view(q), view(k), view(v))
    return o.reshape(b, s, d), lse.reshape(b, s, LANES)


def _mix_out_kernel(x_ref, o0_ref, o1_ref, o2_ref, l0_ref, l1_ref, l2_ref, e_ref, w_ref, y_ref):
    lses = [l0_ref[...], l1_ref[...], l2_ref[...]]
    top = jnp.maximum(jnp.maximum(lses[0], lses[1]), lses[2])
    ws = [jnp.exp(t - top) for t in lses]
    inv = 1.0 / (ws[0] + ws[1] + ws[2])
    e = e_ref[...]
    mixed = jnp.zeros(x_ref.shape, F32)
    for wgt, o_ref in zip(ws, (o0_ref, o1_ref, o2_ref)):
        alpha = wgt * inv
        hi = alpha.astype(BF16)
        lo = (alpha - hi.astype(F32)).astype(BF16)
        spread = (jnp.dot(hi, e, preferred_element_type=F32)
                  + jnp.dot(lo, e, preferred_element_type=F32))
        mixed = mixed + spread * o_ref[...].astype(F32)
    y_ref[...] = x_ref[...] + jnp.dot(mixed.astype(BF16), w_ref[...], preferred_element_type=F32)


def _mix_out(x, outs, lses, w):
    n, d = x.shape
    heads = d // HEAD_DIM
    spread = (jnp.arange(LANES)[:, None] == (jnp.arange(d)[None, :] // HEAD_DIM)).astype(BF16)
    assert heads <= LANES
    row = pl.BlockSpec((ROW_TILE, d), lambda i: (i, 0))
    stat = pl.BlockSpec((ROW_TILE, LANES), lambda i: (i, 0))
    return pl.pallas_call(
        _mix_out_kernel,
        grid=(n // ROW_TILE,),
        in_specs=[row, row, row, row, stat, stat, stat, _resident(spread.shape), _resident(w.shape)],
        out_specs=row,
        out_shape=jax.ShapeDtypeStruct((n, d), F32),
        compiler_params=pltpu.CompilerParams(
            dimension_semantics=("parallel",), vmem_limit_bytes=VMEM_LIMIT),
        name="mix_out_proj",
    )(x, *outs, *lses, spread, w)


def kernel(x, ln_ffn1, w1_gate, w1_up, w1_down, ln_mix, a_w_qkv, a_w_o, a_lambda, a_subln,
           b_w_in, b_w_o, ln_ffn2, w2_gate, w2_up, w2_down, ln_final):
    b, s, d = x.shape
    depth = ln_ffn1.shape[0]
    n = b * s
    tables = _rope_tables(s)
    xf = x.reshape(n, d)
    for i in range(depth):
        xf = _ffn(xf, ln_ffn1[i], w1_gate[i], w1_up[i], w1_down[i])
        j = i // 2
        if i % 2 == 0:
            lambda_init = 0.8 - 0.6 * math.exp(-0.3 * i)
            wq, wk, wv = jnp.split(a_w_qkv[j], 3, axis=1)
            w = jnp.concatenate([_pair_halves_layout(wq), _pair_halves_layout(wk), wv],
                                axis=1).astype(BF16)
            q, k, v = _proj(xf, ln_mix[i], w, tables, ("q", "k", "v"), s, "qkv_proj")
            o = _attn_a(q.reshape(b, s, d), k.reshape(b, s, d), v.reshape(b, s, d),
                        a_lambda[j], a_subln[j], lambda_init)
            xf = _out_proj(xf, o.reshape(n, d), a_w_o[j].astype(BF16))
        else:
            n_groups = len(B_GROUPS)
            parts = jnp.split(b_w_in[j], 2 * n_groups + 1, axis=1)
            w = jnp.concatenate([_pair_halves_layout(p) for p in parts[:-1]] + [parts[-1]],
                                axis=1).astype(BF16)
            proj = _proj(xf, ln_mix[i], w, tables, ("q", "k") * n_groups + ("v",), s, "dilated_proj")
            v = proj[-1].reshape(b, s, d)
            outs, lses = [], []
            for g, (window, dilation) in enumerate(B_GROUPS):
                o, lse = _attn_b_group(proj[2 * g].reshape(b, s, d), proj[2 * g + 1].reshape(b, s, d),
                                       v, dilation, window // (2 * dilation))
                outs.append(o.reshape(n, d))
                lses.append(lse.reshape(n, LANES))
            xf = _mix_out(xf, outs, lses, b_w_o[j].astype(BF16))
        xf = _ffn(xf, ln_ffn2[i], w2_gate[i], w2_up[i], w2_down[i],
                  final_g=ln_final if i == depth - 1 else None)
    return xf.reshape(b, s, d)
```

```python
import functools
import math

import jax
import jax.numpy as jnp
from jax import lax
from jax.experimental import pallas as pl
from jax.experimental.pallas import tpu as pltpu

EPS = 1e-6
ROPE_THETA = 10000.0
NEG_INF = -1e30
HEAD_DIM = 64
ROT_HALF = HEAD_DIM // 2
LANES = 128
B_GROUPS = ((128, 1), (512, 4), (2048, 16))
B_BLOCK = 64

ROW_TILE = 512
FF_CHUNK = 256
A_Q_TILE = 128
A_KEY_CHUNK = 256
B_CHAINS = 8
B_Q_TILE = 128
VMEM_LIMIT = 56 * 1024 * 1024

F32 = jnp.float32
BF16 = jnp.bfloat16


def _resident(shape):
    return pl.BlockSpec(shape, lambda *_: (0,) * len(shape), pipeline_mode=pl.Buffered(1))


def _rms_scale(x):
    return x * lax.rsqrt(jnp.mean(x * x, axis=-1, keepdims=True) + EPS)


def _ffn_kernel(x_ref, g_ref, wg_ref, wu_ref, wd_ref, *rest, n_chunks, final):
    if final:
        gf_ref, o_ref = rest
    else:
        (o_ref,) = rest
    x = x_ref[...]
    h = (_rms_scale(x) * g_ref[...]).astype(BF16)
    acc = jnp.zeros(x.shape, F32)
    for c in range(n_chunks):
        cols = slice(c * FF_CHUNK, (c + 1) * FF_CHUNK)
        gate = jnp.dot(h, wg_ref[:, cols], preferred_element_type=F32)
        up = jnp.dot(h, wu_ref[:, cols], preferred_element_type=F32)
        a = (gate * jax.nn.sigmoid(gate) * up).astype(BF16)
        acc = acc + jnp.dot(a, wd_ref[cols, :], preferred_element_type=F32)
    y = x + 0.5 * acc
    if final:
        y = _rms_scale(y) * gf_ref[...]
    o_ref[...] = y


def _ffn(x, g, wg, wu, wd, final_g=None):
    n, d = x.shape
    ff = wg.shape[1]
    assert n % ROW_TILE == 0 and ff % FF_CHUNK == 0
    final = final_g is not None
    row = pl.BlockSpec((ROW_TILE, d), lambda i: (i, 0))
    in_specs = [row, _resident((1, d)), _resident((d, ff)), _resident((d, ff)), _resident((ff, d))]
    args = [x, g.reshape(1, d), wg.astype(BF16), wu.astype(BF16), wd.astype(BF16)]
    if final:
        in_specs.append(_resident((1, d)))
        args.append(final_g.reshape(1, d))
    return pl.pallas_call(
        functools.partial(_ffn_kernel, n_chunks=ff // FF_CHUNK, final=final),
        grid=(n // ROW_TILE,),
        in_specs=in_specs,
        out_specs=row,
        out_shape=jax.ShapeDtypeStruct((n, d), F32),
        compiler_params=pltpu.CompilerParams(
            dimension_semantics=("parallel",), vmem_limit_bytes=VMEM_LIMIT),
        name="ffn_final" if final else "ffn",
    )(*args)


def _proj_kernel(x_ref, g_ref, w_ref, cq_ref, sq_ref, ck_ref, sk_ref, *rest, chunks):
    out_refs, stage_ref = rest[:-1], rest[-1]
    x = x_ref[...]
    tm, d = x.shape
    h = (_rms_scale(x) * g_ref[...]).astype(BF16)
    o = 0
    for c, (kind, dilations) in enumerate(chunks):
        y = jnp.dot(h, w_ref[:, c * d:(c + 1) * d], preferred_element_type=F32)
        if kind != "v":
            cos = (cq_ref if kind == "q" else ck_ref)[...]
            sin = (sq_ref if kind == "q" else sk_ref)[...]
        for j in range(d // LANES):
            lanes = slice(j * LANES, (j + 1) * LANES)
            r = y[:, lanes]
            if kind != "v":
                r = r * cos + pltpu.roll(r, LANES // 2, 1) * sin
            if any(dil > 1 for dil in dilations):
                stage_ref[j] = r
            for k, dil in enumerate(dilations):
                if dil == 1:
                    out_refs[o + k][:, lanes] = r.astype(BF16)
                    continue
                for p in range(dil):
                    out_refs[o + k][:, p * d + j * LANES:p * d + (j + 1) * LANES] = (
                        stage_ref[j, pl.ds(p, tm // dil, stride=dil), :].astype(BF16))
        o += len(dilations)


def _proj(x, g, w, tables, chunks, seq, name):
    n, d = x.shape
    assert n % ROW_TILE == 0 and seq % ROW_TILE == 0 and w.shape[1] == d * len(chunks)
    pos_blocks = seq // ROW_TILE
    row = pl.BlockSpec((ROW_TILE, d), lambda i: (i, 0))
    tab = pl.BlockSpec((ROW_TILE, LANES), lambda i: (i % pos_blocks, 0))
    dils = [dil for _, ds in chunks for dil in ds]
    return pl.pallas_call(
        functools.partial(_proj_kernel, chunks=chunks),
        grid=(n // ROW_TILE,),
        in_specs=[row, _resident((1, d)), _resident(w.shape), tab, tab, tab, tab],
        out_specs=[pl.BlockSpec((ROW_TILE // dil, dil * d), lambda i: (i, 0)) for dil in dils],
        out_shape=[jax.ShapeDtypeStruct((n // dil, dil * d), BF16) for dil in dils],
        scratch_shapes=[pltpu.VMEM((d // LANES, ROW_TILE, LANES), F32)],
        compiler_params=pltpu.CompilerParams(
            dimension_semantics=("parallel",), vmem_limit_bytes=VMEM_LIMIT),
        name=name,
    )(x, g.reshape(1, d), w, *tables)


def _rope_tables(seq):
    inv = ROPE_THETA ** (-jnp.arange(0, HEAD_DIM, 2, dtype=F32) / HEAD_DIM)
    ang = jnp.arange(seq, dtype=F32)[:, None] * inv[None, :]
    cos, sin = jnp.cos(ang), jnp.sin(ang)
    cos_t = jnp.concatenate([cos, cos, cos, cos], axis=1)
    sin_t = jnp.concatenate([-sin, -sin, sin, sin], axis=1)
    scale = HEAD_DIM ** -0.5 * math.log2(math.e)
    return cos_t * scale, sin_t * scale, cos_t, sin_t


def _pair_halves_layout(w_chunk):
    din, dout = w_chunk.shape
    w5 = w_chunk.reshape(din, dout // LANES, 2, 2, ROT_HALF)
    return w5.transpose(0, 1, 3, 2, 4).reshape(din, dout)


def _stack_pair(q):
    lane = lax.broadcasted_iota(jnp.int32, q.shape, 1)
    first = (lane % HEAD_DIM) < ROT_HALF
    zero = jnp.zeros_like(q)
    return jnp.concatenate([jnp.where(first, q, zero), jnp.where(first, zero, q)], axis=0)


def _attn_a_kernel(lam_ref, subln_ref, q_ref, k_ref, v_ref, o_ref,
                   s0_ref, s1_ref, m0_ref, m1_ref, *, lambda_init, tq):
    n_keys = k_ref.shape[1]
    n_tiles = q_ref.shape[1] // tq
    lam = lam_ref[...]
    lam_full = (jnp.exp(jnp.sum(lam[0:1] * lam[1:2], axis=-1, keepdims=True))
                - jnp.exp(jnp.sum(lam[2:3] * lam[3:4], axis=-1, keepdims=True)) + lambda_init)

    def stage(t_new, new_refs, t_old, old_refs):
        n_chunks = n_keys // A_KEY_CHUNK
        groups = A_KEY_CHUNK // LANES
        if t_new is not None:
            s_new, m_new = new_refs
            q2 = _stack_pair(q_ref[0, pl.ds(pl.multiple_of(t_new * tq, tq), tq), :])
            top = jnp.full((2 * tq, LANES), -jnp.inf, F32)
        if t_old is not None:
            s_old, m_old = old_refs
            m = m_old[...]
            ones = jnp.ones((A_KEY_CHUNK, LANES), BF16)
            pv = jnp.zeros((2 * tq, 2 * LANES), F32)
        for kb in range(n_chunks):
            keys = slice(kb * A_KEY_CHUNK, (kb + 1) * A_KEY_CHUNK)
            if t_new is not None:
                s = lax.dot_general(q2, k_ref[0, keys, :], (((1,), (1,)), ((), ())),
                                    preferred_element_type=F32)
                s_new[:, keys] = s
                for g in range(groups):
                    top = jnp.maximum(top, s[:, g * LANES:(g + 1) * LANES])
            if t_old is not None:
                parts = []
                for g in range(groups):
                    c0 = kb * A_KEY_CHUNK + g * LANES
                    p = jnp.exp2(s_old[:, c0:c0 + LANES] - m)
                    parts.append(p.astype(BF16))
                v_ones = jnp.concatenate([v_ref[0, keys, :], ones], axis=1)
                pv = pv + jnp.dot(jnp.concatenate(parts, axis=1), v_ones,
                                  preferred_element_type=F32)
        if t_new is not None:
            m_new[...] = jnp.broadcast_to(jnp.max(top, axis=-1, keepdims=True), m_new.shape)
        if t_old is not None:
            pv = pv[:, :LANES] / pv[:, LANES:]
            o = pv[:tq] - lam_full * pv[tq:]
            o = _rms_scale(o) * subln_ref[...] * (1.0 - lambda_init)
            o_ref[0, pl.ds(pl.multiple_of(t_old * tq, tq), tq), :] = o.astype(BF16)

    buf0, buf1 = (s0_ref, m0_ref), (s1_ref, m1_ref)
    stage(0, buf0, None, None)

    def pair(u, carry):
        t = 2 * u
        stage(t + 1, buf1, t, buf0)
        stage(t + 2, buf0, t + 1, buf1)
        return carry

    lax.fori_loop(0, n_tiles // 2 - 1, pair, 0)
    stage(n_tiles - 1, buf1, n_tiles - 2, buf0)
    stage(None, None, n_tiles - 1, buf1)


def _attn_a(q, k, v, lam, subln, lambda_init):
    b, s, d = q.shape
    heads = d // LANES
    tq = A_Q_TILE
    assert s % (2 * tq) == 0
    blk = pl.BlockSpec((1, s, LANES), lambda bi, hi: (bi, 0, hi))
    return pl.pallas_call(
        functools.partial(_attn_a_kernel, lambda_init=lambda_init, tq=tq),
        grid=(b, heads),
        in_specs=[_resident(lam.shape), _resident((1, LANES)), blk, blk, blk],
        out_specs=blk,
        out_shape=jax.ShapeDtypeStruct((b, s, d), BF16),
        scratch_shapes=[pltpu.VMEM((2 * tq, s), F32), pltpu.VMEM((2 * tq, s), F32),
                        pltpu.VMEM((2 * tq, LANES), F32), pltpu.VMEM((2 * tq, LANES), F32)],
        compiler_params=pltpu.CompilerParams(
            dimension_semantics=("parallel", "parallel"), vmem_limit_bytes=VMEM_LIMIT),
        name="diff_attn",
    )(lam, subln.reshape(1, LANES), q, k, v)


def _out_kernel(x_ref, o_ref, w_ref, y_ref):
    y_ref[...] = x_ref[...] + jnp.dot(o_ref[...], w_ref[...], preferred_element_type=F32)


def _out_proj(x, o, w):
    n, d = x.shape
    row = pl.BlockSpec((ROW_TILE, d), lambda i: (i, 0))
    return pl.pallas_call(
        _out_kernel,
        grid=(n // ROW_TILE,),
        in_specs=[row, row, _resident(w.shape)],
        out_specs=row,
        out_shape=jax.ShapeDtypeStruct((n, d), F32),
        compiler_params=pltpu.CompilerParams(
            dimension_semantics=("parallel",), vmem_limit_bytes=VMEM_LIMIT),
        name="out_proj",
    )(x, o, w)


def _attn_b_kernel(q_ref, k_ref, v_ref, o_ref, lse_ref, *, seq_len, half, blocks_per_phase):
    cw = q_ref.shape[2]
    tq = min(B_Q_TILE, seq_len)
    win = tq + 2 * half
    n_pairs = cw // LANES
    j = pl.program_id(1)
    first_pair = (j % blocks_per_phase) * n_pairs

    if blocks_per_phase > 1:
        @pl.when(j % blocks_per_phase == 0)
        def _():
            lse_ref[...] = jnp.zeros(lse_ref.shape, F32)

    lane = lax.broadcasted_iota(jnp.int32, (tq, LANES), 1)
    row = lax.broadcasted_iota(jnp.int32, (2 * tq, win), 0) % tq
    col = lax.broadcasted_iota(jnp.int32, (2 * tq, win), 1)
    ones = jnp.ones((win, LANES), BF16)

    def tile(t, carry):
        i0 = pl.multiple_of(t * tq, tq)
        w0 = pl.multiple_of(jnp.clip(i0 - half, 0, seq_len - win), half)
        valid = jnp.abs((row + i0) - (col + w0)) <= half
        if blocks_per_phase > 1:
            lse_tile = lse_ref[0, pl.ds(i0, tq), :]
        else:
            lse_tile = jnp.zeros((tq, LANES), F32)
        for pr in range(n_pairs):
            cols = slice(pr * LANES, (pr + 1) * LANES)
            lane_a = 2 * (first_pair + pr)
            q2 = _stack_pair(q_ref[0, pl.ds(i0, tq), cols])
            kw = k_ref[0, pl.ds(w0, win), cols]
            vw = v_ref[0, pl.ds(w0, win), cols]
            s = lax.dot_general(q2, kw, (((1,), (1,)), ((), ())),
                                preferred_element_type=F32)
            s = jnp.where(valid, s, NEG_INF)
            m = jnp.max(s, axis=-1, keepdims=True)
            p = jnp.exp2(s - m)
            pvl = jnp.dot(p.astype(BF16), jnp.concatenate([vw, ones], axis=1),
                          preferred_element_type=F32)
            l = pvl[:, LANES:]
            pv = pvl[:, :LANES] / l
            o_ref[0, pl.ds(i0, tq), cols] = jnp.where(
                lane < HEAD_DIM, pv[:tq], pv[tq:]).astype(BF16)
            lse = m + jnp.log2(l)
            lse_tile = jnp.where(lane == lane_a, lse[:tq],
                                 jnp.where(lane == lane_a + 1, lse[tq:], lse_tile))
        lse_ref[0, pl.ds(i0, tq), :] = lse_tile
        return carry

    n_tiles = seq_len // tq
    lax.fori_loop(0, n_tiles, tile, 0, unroll=min(n_tiles, max(1, B_CHAINS // n_pairs)))


def _attn_b_group(q, k, v, b, dilation, half):
    d = q.shape[1] // dilation
    sl = q.shape[0] // b
    tq = min(B_Q_TILE, sl)
    assert sl % tq == 0 and sl >= tq + 2 * half
    cw = max(LANES, min(d, (1 << 20) // sl))
    assert d % cw == 0
    blocks_per_phase = d // cw
    view = lambda t: t.reshape(b, sl, dilation * d)
    blk = pl.BlockSpec((1, sl, cw), lambda bi, j: (bi, 0, j))
    lse_blk = pl.BlockSpec((1, sl, LANES), lambda bi, j: (bi, 0, j // blocks_per_phase))
    o, lse = pl.pallas_call(
        functools.partial(_attn_b_kernel, seq_len=sl, half=half, blocks_per_phase=blocks_per_phase),
        grid=(b, dilation * blocks_per_phase),
        in_specs=[blk, blk, blk],
        out_specs=[blk, lse_blk],
        out_shape=[jax.ShapeDtypeStruct((b, sl, dilation * d), BF16),
                   jax.ShapeDtypeStruct((b, sl, dilation * LANES), F32)],
        compiler_params=pltpu.CompilerParams(
            dimension_semantics=("parallel", "arbitrary"), vmem_limit_bytes=VMEM_LIMIT),
        name=f"dilated_attn_d{dilation}",
    )(view(q), view(k), view(v))
    return o.reshape(b * sl, dilation * d), lse.reshape(b * sl, dilation * LANES)


def _mix_out_kernel(x_ref, *rest, dilations):
    n_g = len(dilations)
    o_refs, l_refs = rest[:n_g], rest[n_g:2 * n_g]
    e_ref, w_ref, y_ref = rest[2 * n_g:2 * n_g + 3]
    stages = list(rest[2 * n_g + 3:])
    tm, d = x_ref.shape
    n_lanes = d // LANES

    lses, o_stage = [], []
    for o_ref, l_ref, dil in zip(o_refs, l_refs, dilations):
        if dil == 1:
            lses.append(l_ref[...])
            o_stage.append(None)
            continue
        ost, lst = stages.pop(0), stages.pop(0)
        for p in range(dil):
            rows = pl.ds(p, tm // dil, stride=dil)
            lst[rows, :] = l_ref[:, p * LANES:(p + 1) * LANES]
            for j in range(n_lanes):
                ost[j, rows, :] = o_ref[:, p * d + j * LANES:p * d + (j + 1) * LANES].astype(F32)
        lses.append(lst[...])
        o_stage.append(ost)

    top = functools.reduce(jnp.maximum, lses)
    ws = [jnp.exp2(t - top) for t in lses]
    inv = 1.0 / functools.reduce(jnp.add, ws)
    e = e_ref[...]
    spreads = []
    for wgt in ws:
        alpha = wgt * inv
        hi = alpha.astype(BF16)
        lo = (alpha - hi.astype(F32)).astype(BF16)
        spreads.append(jnp.dot(hi, e, preferred_element_type=F32)
                       + jnp.dot(lo, e, preferred_element_type=F32))
    parts = []
    for j in range(n_lanes):
        lanes = slice(j * LANES, (j + 1) * LANES)
        acc = jnp.zeros((tm, LANES), F32)
        for o_ref, ost, spread in zip(o_refs, o_stage, spreads):
            og = o_ref[:, lanes].astype(F32) if ost is None else ost[j]
            acc = acc + spread[:, lanes] * og
        parts.append(acc.astype(BF16))
    y_ref[...] = x_ref[...] + jnp.dot(jnp.concatenate(parts, axis=1), w_ref[...],
                                      preferred_element_type=F32)


def _mix_out(x, outs, lses, dilations, w):
    n, d = x.shape
    assert d // HEAD_DIM <= LANES
    spread = (jnp.arange(LANES)[:, None] == (jnp.arange(d)[None, :] // HEAD_DIM)).astype(BF16)
    row = pl.BlockSpec((ROW_TILE, d), lambda i: (i, 0))
    o_specs = [pl.BlockSpec((ROW_TILE // dil, dil * d), lambda i: (i, 0)) for dil in dilations]
    l_specs = [pl.BlockSpec((ROW_TILE // dil, dil * LANES), lambda i: (i, 0)) for dil in dilations]
    scratch = []
    for dil in dilations:
        if dil > 1:
            scratch += [pltpu.VMEM((d // LANES, ROW_TILE, LANES), F32),
                        pltpu.VMEM((ROW_TILE, LANES), F32)]
    return pl.pallas_call(
        functools.partial(_mix_out_kernel, dilations=tuple(dilations)),
        grid=(n // ROW_TILE,),
        in_specs=[row, *o_specs, *l_specs, _resident(spread.shape), _resident(w.shape)],
        out_specs=row,
        out_shape=jax.ShapeDtypeStruct((n, d), F32),
        scratch_shapes=scratch,
        compiler_params=pltpu.CompilerParams(
            dimension_semantics=("parallel",), vmem_limit_bytes=VMEM_LIMIT),
        name="mix_out_proj",
    )(x, *outs, *lses, spread, w)


def kernel(x, ln_ffn1, w1_gate, w1_up, w1_down, ln_mix, a_w_qkv, a_w_o, a_lambda, a_subln,
           b_w_in, b_w_o, ln_ffn2, w2_gate, w2_up, w2_down, ln_final):
    b, s, d = x.shape
    depth = ln_ffn1.shape[0]
    n = b * s
    tables = _rope_tables(s)
    xf = x.reshape(n, d)
    for i in range(depth):
        xf = _ffn(xf, ln_ffn1[i], w1_gate[i], w1_up[i], w1_down[i])
        j = i // 2
        if i % 2 == 0:
            lambda_init = 0.8 - 0.6 * math.exp(-0.3 * i)
            wq, wk, wv = jnp.split(a_w_qkv[j], 3, axis=1)
            w = jnp.concatenate([_pair_halves_layout(wq), _pair_halves_layout(wk), wv],
                                axis=1).astype(BF16)
            q, k, v = _proj(xf, ln_mix[i], w, tables, (("q", (1,)), ("k", (1,)), ("v", (1,))),
                            s, "qkv_proj")
            o = _attn_a(q.reshape(b, s, d), k.reshape(b, s, d), v.reshape(b, s, d),
                        a_lambda[j], a_subln[j], lambda_init)
            xf = _out_proj(xf, o.reshape(n, d), a_w_o[j].astype(BF16))
        else:
            n_groups = len(B_GROUPS)
            parts = jnp.split(b_w_in[j], 2 * n_groups + 1, axis=1)
            w = jnp.concatenate([_pair_halves_layout(p) for p in parts[:-1]] + [parts[-1]],
                                axis=1).astype(BF16)
            dilations = tuple(dil for _, dil in B_GROUPS)
            chunks = sum(((("q", (dil,)), ("k", (dil,))) for dil in dilations), ())
            chunks += (("v", dilations),)
            proj = _proj(xf, ln_mix[i], w, tables, chunks, s, "dilated_proj")
            outs, lses = [], []
            for g, (window, dilation) in enumerate(B_GROUPS):
                o, lse = _attn_b_group(proj[2 * g], proj[2 * g + 1], proj[2 * n_groups + g],
                                       b, dilation, window // (2 * dilation))
                outs.append(o)
                lses.append(lse)
            xf = _mix_out(xf, outs, lses, dilations, b_w_o[j].astype(BF16))
        xf = _ffn(xf, ln_ffn2[i], w2_gate[i], w2_up[i], w2_down[i],
                  final_g=ln_final if i == depth - 1 else None)
    return xf.reshape(b, s, d)
```

```python
import functools
import math

import jax
import jax.numpy as jnp
from jax import lax
from jax.experimental import pallas as pl
from jax.experimental.pallas import tpu as pltpu

EPS = 1e-6
ROPE_THETA = 10000.0
NEG_INF = -1e30
HEAD_DIM = 64
ROT_HALF = HEAD_DIM // 2
LANES = 128
B_GROUPS = ((128, 1), (512, 4), (2048, 16))
B_BLOCK = 64

ROW_TILE = 512
FF_CHUNK = 256
A_Q_TILE = 128
A_KEY_CHUNK = 256
A_STEPS_PER_ITER = 6
B_CHAINS = 8
B_Q_TILE = 128
VMEM_LIMIT = 56 * 1024 * 1024

F32 = jnp.float32
BF16 = jnp.bfloat16


def _resident(shape):
    return pl.BlockSpec(shape, lambda *_: (0,) * len(shape), pipeline_mode=pl.Buffered(1))


def _rms_scale(x):
    return x * lax.rsqrt(jnp.mean(x * x, axis=-1, keepdims=True) + EPS)


def _ffn_kernel(x_ref, g_ref, wg_ref, wu_ref, wd_ref, *rest, n_chunks, final):
    if final:
        gf_ref, o_ref = rest
    else:
        (o_ref,) = rest
    x = x_ref[...]
    h = (_rms_scale(x) * g_ref[...]).astype(BF16)
    acc = jnp.zeros(x.shape, F32)
    for c in range(n_chunks):
        cols = slice(c * FF_CHUNK, (c + 1) * FF_CHUNK)
        gate = jnp.dot(h, wg_ref[:, cols], preferred_element_type=F32)
        up = jnp.dot(h, wu_ref[:, cols], preferred_element_type=F32)
        a = (gate * jax.nn.sigmoid(gate) * up).astype(BF16)
        acc = acc + jnp.dot(a, wd_ref[cols, :], preferred_element_type=F32)
    y = x + 0.5 * acc
    if final:
        y = _rms_scale(y) * gf_ref[...]
    o_ref[...] = y


def _ffn(x, g, wg, wu, wd, final_g=None):
    n, d = x.shape
    ff = wg.shape[1]
    assert n % ROW_TILE == 0 and ff % FF_CHUNK == 0
    final = final_g is not None
    row = pl.BlockSpec((ROW_TILE, d), lambda i: (i, 0))
    in_specs = [row, _resident((1, d)), _resident((d, ff)), _resident((d, ff)), _resident((ff, d))]
    args = [x, g.reshape(1, d), wg.astype(BF16), wu.astype(BF16), wd.astype(BF16)]
    if final:
        in_specs.append(_resident((1, d)))
        args.append(final_g.reshape(1, d))
    return pl.pallas_call(
        functools.partial(_ffn_kernel, n_chunks=ff // FF_CHUNK, final=final),
        grid=(n // ROW_TILE,),
        in_specs=in_specs,
        out_specs=row,
        out_shape=jax.ShapeDtypeStruct((n, d), F32),
        compiler_params=pltpu.CompilerParams(
            dimension_semantics=("parallel",), vmem_limit_bytes=VMEM_LIMIT),
        name="ffn_final" if final else "ffn",
    )(*args)


STRIDED_LOAD_MAX = 4


def _phase_pitch(rows):
    return rows if (rows // 8) % 2 else rows + 8


def _proj_kernel(x_ref, g_ref, w_ref, cq_ref, sq_ref, ck_ref, sk_ref, *rest, chunks):
    out_refs, stage_ref, wide_ref = rest[:-2], rest[-2], rest[-1]
    x = x_ref[...]
    tm, d = x.shape
    h = (_rms_scale(x) * g_ref[...]).astype(BF16)
    o = 0
    for c, (kind, dilations) in enumerate(chunks):
        y = jnp.dot(h, w_ref[:, c * d:(c + 1) * d], preferred_element_type=F32)
        if kind != "v":
            cos = (cq_ref if kind == "q" else ck_ref)[...]
            sin = (sq_ref if kind == "q" else sk_ref)[...]
        for j in range(d // LANES):
            lanes = slice(j * LANES, (j + 1) * LANES)
            r = y[:, lanes]
            if kind != "v":
                r = r * cos + pltpu.roll(r, LANES // 2, 1) * sin
            if any(1 < dil <= STRIDED_LOAD_MAX for dil in dilations):
                stage_ref[j] = r
            for k, dil in enumerate(dilations):
                rows = tm // dil
                if dil == 1:
                    out_refs[o + k][:, lanes] = r.astype(BF16)
                    continue
                if dil <= STRIDED_LOAD_MAX:
                    for p in range(dil):
                        out_refs[o + k][:, p * d + j * LANES:p * d + (j + 1) * LANES] = (
                            stage_ref[j, pl.ds(p, rows, stride=dil), :].astype(BF16))
                    continue
                pitch = _phase_pitch(rows)
                for m0 in range(rows):
                    wide_ref[j, pl.ds(m0, dil, stride=pitch), :] = r[m0 * dil:(m0 + 1) * dil, :]
                for p in range(dil):
                    out_refs[o + k][:, p * d + j * LANES:p * d + (j + 1) * LANES] = (
                        wide_ref[j, p * pitch:p * pitch + rows, :].astype(BF16))
        o += len(dilations)


def _proj(x, g, w, tables, chunks, seq, name):
    n, d = x.shape
    assert n % ROW_TILE == 0 and seq % ROW_TILE == 0 and w.shape[1] == d * len(chunks)
    pos_blocks = seq // ROW_TILE
    row = pl.BlockSpec((ROW_TILE, d), lambda i: (i, 0))
    tab = pl.BlockSpec((ROW_TILE, LANES), lambda i: (i % pos_blocks, 0))
    dils = [dil for _, ds in chunks for dil in ds]
    wide_rows = max([dil * _phase_pitch(ROW_TILE // dil) for dil in dils if dil > STRIDED_LOAD_MAX],
                    default=8)
    return pl.pallas_call(
        functools.partial(_proj_kernel, chunks=chunks),
        grid=(n // ROW_TILE,),
        in_specs=[row, _resident((1, d)), _resident(w.shape), tab, tab, tab, tab],
        out_specs=[pl.BlockSpec((ROW_TILE // dil, dil * d), lambda i: (i, 0)) for dil in dils],
        out_shape=[jax.ShapeDtypeStruct((n // dil, dil * d), BF16) for dil in dils],
        scratch_shapes=[pltpu.VMEM((d // LANES, ROW_TILE, LANES), F32),
                        pltpu.VMEM((d // LANES, wide_rows, LANES), F32)],
        compiler_params=pltpu.CompilerParams(
            dimension_semantics=("parallel",), vmem_limit_bytes=VMEM_LIMIT),
        name=name,
    )(x, g.reshape(1, d), w, *tables)


def _rope_tables(seq):
    inv = ROPE_THETA ** (-jnp.arange(0, HEAD_DIM, 2, dtype=F32) / HEAD_DIM)
    ang = jnp.arange(seq, dtype=F32)[:, None] * inv[None, :]
    cos, sin = jnp.cos(ang), jnp.sin(ang)
    cos_t = jnp.concatenate([cos, cos, cos, cos], axis=1)
    sin_t = jnp.concatenate([-sin, -sin, sin, sin], axis=1)
    scale = HEAD_DIM ** -0.5 * math.log2(math.e)
    return cos_t * scale, sin_t * scale, cos_t, sin_t


def _pair_halves_layout(w_chunk):
    din, dout = w_chunk.shape
    w5 = w_chunk.reshape(din, dout // LANES, 2, 2, ROT_HALF)
    return w5.transpose(0, 1, 3, 2, 4).reshape(din, dout)


def _stack_pair(q):
    lane = lax.broadcasted_iota(jnp.int32, q.shape, 1)
    first = (lane % HEAD_DIM) < ROT_HALF
    zero = jnp.zeros_like(q)
    return jnp.concatenate([jnp.where(first, q, zero), jnp.where(first, zero, q)], axis=0)


def _attn_a_kernel(lam_ref, subln_ref, q_ref, k_ref, v_ref, o_ref,
                   s0_ref, s1_ref, m0_ref, m1_ref, acc0_ref, acc1_ref, vt_ref, *, lambda_init, tq):
    n_keys = k_ref.shape[1]
    n_tiles = q_ref.shape[1] // tq
    lam = lam_ref[...]
    lam_full = (jnp.exp(jnp.sum(lam[0:1] * lam[1:2], axis=-1, keepdims=True))
                - jnp.exp(jnp.sum(lam[2:3] * lam[3:4], axis=-1, keepdims=True)) + lambda_init)

    sub = 8
    n_chunks = n_keys // A_KEY_CHUNK

    bufs = ((s0_ref, m0_ref, acc0_ref), (s1_ref, m1_ref, acc1_ref))

    def step(i, parity, scores=True, numer=True, final=True):
        s_new, m_new, acc_fin = bufs[parity]
        s_old, m_old, acc_out = bufs[1 - parity]
        t_new = i if scores else None
        t_old = i - 1 if numer else None
        if final:
            saved = acc_fin[...]
            pv = saved[:LANES] / saved[LANES:LANES + 1]
            o = (pv[:, :tq] - lam_full * pv[:, tq:]).T
            o = _rms_scale(o) * subln_ref[...] * (1.0 - lambda_init)
            o_ref[0, pl.ds(pl.multiple_of((i - 2) * tq, tq), tq), :] = o.astype(BF16)
        if t_new is not None:
            q2t = _stack_pair(q_ref[0, pl.ds(pl.multiple_of(t_new * tq, tq), tq), :]).T
            top = jnp.full((sub, 2 * tq), -jnp.inf, F32)
        if t_old is not None:
            m = m_old[...]
            acc = jnp.zeros((vt_ref.shape[0], 2 * tq), F32)
        for kb in range(n_chunks):
            keys = slice(kb * A_KEY_CHUNK, (kb + 1) * A_KEY_CHUNK)
            if t_new is not None:
                st = jnp.dot(k_ref[0, keys, :], q2t, preferred_element_type=F32)
                s_new[keys, :] = st
                top = jnp.maximum(top, jnp.max(st.reshape(A_KEY_CHUNK // sub, sub, 2 * tq), axis=0))
            if t_old is not None:
                e = s_old[keys, :].reshape(A_KEY_CHUNK // sub, sub, 2 * tq) - m[None]
                pt = jnp.exp2(e).reshape(A_KEY_CHUNK, 2 * tq).astype(BF16)
                acc = acc + jnp.dot(vt_ref[:, keys], pt, preferred_element_type=F32)
        if t_new is not None:
            m_new[...] = jnp.broadcast_to(jnp.max(top, axis=0, keepdims=True), m_new.shape)
        if t_old is not None:
            acc_out[...] = acc

    vt_ref[:LANES, :] = v_ref[0].T
    vt_ref[LANES:, :] = jnp.ones((vt_ref.shape[0] - LANES, n_keys), BF16)

    step(0, 0, numer=False, final=False)
    step(1, 1, final=False)

    def steps(u, carry):
        for k in range(A_STEPS_PER_ITER):
            step(A_STEPS_PER_ITER * u + 2 + k, k % 2)
        return carry

    assert (n_tiles - 2) % A_STEPS_PER_ITER == 0 and A_STEPS_PER_ITER % 2 == 0
    lax.fori_loop(0, (n_tiles - 2) // A_STEPS_PER_ITER, steps, 0)
    step(n_tiles, 0, scores=False)
    step(n_tiles + 1, 1, scores=False, numer=False)


def _attn_a(q, k, v, lam, subln, lambda_init):
    b, s, d = q.shape
    heads = d // LANES
    tq = A_Q_TILE
    assert s % (2 * tq) == 0
    blk = pl.BlockSpec((1, s, LANES), lambda bi, hi: (bi, 0, hi))
    return pl.pallas_call(
        functools.partial(_attn_a_kernel, lambda_init=lambda_init, tq=tq),
        grid=(b, heads),
        in_specs=[_resident(lam.shape), _resident((1, LANES)), blk, blk, blk],
        out_specs=blk,
        out_shape=jax.ShapeDtypeStruct((b, s, d), BF16),
        scratch_shapes=[pltpu.VMEM((s, 2 * tq), F32), pltpu.VMEM((s, 2 * tq), F32),
                        pltpu.VMEM((8, 2 * tq), F32), pltpu.VMEM((8, 2 * tq), F32),
                        pltpu.VMEM((LANES + 16, 2 * tq), F32), pltpu.VMEM((LANES + 16, 2 * tq), F32),
                        pltpu.VMEM((LANES + 16, s), BF16)],
        compiler_params=pltpu.CompilerParams(
            dimension_semantics=("parallel", "parallel"), vmem_limit_bytes=VMEM_LIMIT),
        name="diff_attn",
    )(lam, subln.reshape(1, LANES), q, k, v)


def _out_kernel(x_ref, o_ref, w_ref, y_ref):
    y_ref[...] = x_ref[...] + jnp.dot(o_ref[...], w_ref[...], preferred_element_type=F32)


def _out_proj(x, o, w):
    n, d = x.shape
    row = pl.BlockSpec((ROW_TILE, d), lambda i: (i, 0))
    return pl.pallas_call(
        _out_kernel,
        grid=(n // ROW_TILE,),
        in_specs=[row, row, _resident(w.shape)],
        out_specs=row,
        out_shape=jax.ShapeDtypeStruct((n, d), F32),
        compiler_params=pltpu.CompilerParams(
            dimension_semantics=("parallel",), vmem_limit_bytes=VMEM_LIMIT),
        name="out_proj",
    )(x, o, w)


def _attn_b_kernel(q_ref, k_ref, v_ref, o_ref, lse_ref, *, seq_len, half, blocks_per_phase):
    cw = q_ref.shape[2]
    tq = min(B_Q_TILE, seq_len)
    win = tq + 2 * half
    n_pairs = cw // LANES
    j = pl.program_id(1)
    first_pair = (j % blocks_per_phase) * n_pairs

    if blocks_per_phase > 1:
        @pl.when(j % blocks_per_phase == 0)
        def _():
            lse_ref[...] = jnp.zeros(lse_ref.shape, F32)

    lane = lax.broadcasted_iota(jnp.int32, (tq, LANES), 1)
    row = lax.broadcasted_iota(jnp.int32, (2 * tq, win), 0) % tq
    col = lax.broadcasted_iota(jnp.int32, (2 * tq, win), 1)
    ones = jnp.ones((win, LANES), BF16)

    def tile(t, carry):
        i0 = pl.multiple_of(t * tq, tq)
        w0 = pl.multiple_of(jnp.clip(i0 - half, 0, seq_len - win), half)
        valid = jnp.abs((row + i0) - (col + w0)) <= half
        if blocks_per_phase > 1:
            lse_tile = lse_ref[0, pl.ds(i0, tq), :]
        else:
            lse_tile = jnp.zeros((tq, LANES), F32)
        for pr in range(n_pairs):
            cols = slice(pr * LANES, (pr + 1) * LANES)
            lane_a = 2 * (first_pair + pr)
            q2 = _stack_pair(q_ref[0, pl.ds(i0, tq), cols])
            kw = k_ref[0, pl.ds(w0, win), cols]
            vw = v_ref[0, pl.ds(w0, win), cols]
            s = lax.dot_general(q2, kw, (((1,), (1,)), ((), ())),
                                preferred_element_type=F32)
            s = jnp.where(valid, s, NEG_INF)
            m = jnp.max(s, axis=-1, keepdims=True)
            p = jnp.exp2(s - m)
            pvl = jnp.dot(p.astype(BF16), jnp.concatenate([vw, ones], axis=1),
                          preferred_element_type=F32)
            l = pvl[:, LANES:]
            pv = pvl[:, :LANES] / l
            o_ref[0, pl.ds(i0, tq), cols] = jnp.where(
                lane < HEAD_DIM, pv[:tq], pv[tq:]).astype(BF16)
            lse = m + jnp.log2(l)
            lse_tile = jnp.where(lane == lane_a, lse[:tq],
                                 jnp.where(lane == lane_a + 1, lse[tq:], lse_tile))
        lse_ref[0, pl.ds(i0, tq), :] = lse_tile
        return carry

    n_tiles = seq_len // tq
    lax.fori_loop(0, n_tiles, tile, 0, unroll=min(n_tiles, max(1, B_CHAINS // n_pairs)))


def _attn_b_group(q, k, v, b, dilation, half):
    d = q.shape[1] // dilation
    sl = q.shape[0] // b
    tq = min(B_Q_TILE, sl)
    assert sl % tq == 0 and sl >= tq + 2 * half
    cw = max(LANES, min(d, (1 << 20) // sl))
    assert d % cw == 0
    blocks_per_phase = d // cw
    view = lambda t: t.reshape(b, sl, dilation * d)
    blk = pl.BlockSpec((1, sl, cw), lambda bi, j: (bi, 0, j))
    lse_blk = pl.BlockSpec((1, sl, LANES), lambda bi, j: (bi, 0, j // blocks_per_phase))
    o, lse = pl.pallas_call(
        functools.partial(_attn_b_kernel, seq_len=sl, half=half, blocks_per_phase=blocks_per_phase),
        grid=(b, dilation * blocks_per_phase),
        in_specs=[blk, blk, blk],
        out_specs=[blk, lse_blk],
        out_shape=[jax.ShapeDtypeStruct((b, sl, dilation * d), BF16),
                   jax.ShapeDtypeStruct((b, sl, dilation * LANES), F32)],
        compiler_params=pltpu.CompilerParams(
            dimension_semantics=("parallel", "arbitrary"), vmem_limit_bytes=VMEM_LIMIT),
        name=f"dilated_attn_d{dilation}",
    )(view(q), view(k), view(v))
    return o.reshape(b * sl, dilation * d), lse.reshape(b * sl, dilation * LANES)


def _mix_out_kernel(x_ref, *rest, dilations):
    n_g = len(dilations)
    o_refs, l_refs = rest[:n_g], rest[n_g:2 * n_g]
    e_ref, w_ref, y_ref = rest[2 * n_g:2 * n_g + 3]
    stages = list(rest[2 * n_g + 3:])
    tm, d = x_ref.shape
    n_lanes = d // LANES

    lses, o_stage = [], []
    for o_ref, l_ref, dil in zip(o_refs, l_refs, dilations):
        if dil == 1:
            lses.append(l_ref[...])
            o_stage.append(None)
            continue
        ost, lst = stages.pop(0), stages.pop(0)
        for p in range(dil):
            rows = pl.ds(p, tm // dil, stride=dil)
            lst[rows, :] = l_ref[:, p * LANES:(p + 1) * LANES]
            for j in range(n_lanes):
                ost[j, rows, :] = o_ref[:, p * d + j * LANES:p * d + (j + 1) * LANES].astype(F32)
        lses.append(lst[...])
        o_stage.append(ost)

    top = functools.reduce(jnp.maximum, lses)
    ws = [jnp.exp2(t - top) for t in lses]
    inv = 1.0 / functools.reduce(jnp.add, ws)
    e = e_ref[...]
    spreads = []
    for wgt in ws:
        alpha = wgt * inv
        hi = alpha.astype(BF16)
        lo = (alpha - hi.astype(F32)).astype(BF16)
        spreads.append(jnp.dot(hi, e, preferred_element_type=F32)
                       + jnp.dot(lo, e, preferred_element_type=F32))
    parts = []
    for j in range(n_lanes):
        lanes = slice(j * LANES, (j + 1) * LANES)
        acc = jnp.zeros((tm, LANES), F32)
        for o_ref, ost, spread in zip(o_refs, o_stage, spreads):
            og = o_ref[:, lanes].astype(F32) if ost is None else ost[j]
            acc = acc + spread[:, lanes] * og
        parts.append(acc.astype(BF16))
    y_ref[...] = x_ref[...] + jnp.dot(jnp.concatenate(parts, axis=1), w_ref[...],
                                      preferred_element_type=F32)


def _mix_out(x, outs, lses, dilations, w):
    n, d = x.shape
    assert d // HEAD_DIM <= LANES
    spread = (jnp.arange(LANES)[:, None] == (jnp.arange(d)[None, :] // HEAD_DIM)).astype(BF16)
    row = pl.BlockSpec((ROW_TILE, d), lambda i: (i, 0))
    o_specs = [pl.BlockSpec((ROW_TILE // dil, dil * d), lambda i: (i, 0)) for dil in dilations]
    l_specs = [pl.BlockSpec((ROW_TILE // dil, dil * LANES), lambda i: (i, 0)) for dil in dilations]
    scratch = []
    for dil in dilations:
        if dil > 1:
            scratch += [pltpu.VMEM((d // LANES, ROW_TILE, LANES), F32),
                        pltpu.VMEM((ROW_TILE, LANES), F32)]
    return pl.pallas_call(
        functools.partial(_mix_out_kernel, dilations=tuple(dilations)),
        grid=(n // ROW_TILE,),
        in_specs=[row, *o_specs, *l_specs, _resident(spread.shape), _resident(w.shape)],
        out_specs=row,
        out_shape=jax.ShapeDtypeStruct((n, d), F32),
        scratch_shapes=scratch,
        compiler_params=pltpu.CompilerParams(
            dimension_semantics=("parallel",), vmem_limit_bytes=VMEM_LIMIT),
        name="mix_out_proj",
    )(x, *outs, *lses, spread, w)


def kernel(x, ln_ffn1, w1_gate, w1_up, w1_down, ln_mix, a_w_qkv, a_w_o, a_lambda, a_subln,
           b_w_in, b_w_o, ln_ffn2, w2_gate, w2_up, w2_down, ln_final):
    b, s, d = x.shape
    depth = ln_ffn1.shape[0]
    n = b * s
    tables = _rope_tables(s)
    xf = x.reshape(n, d)
    for i in range(depth):
        xf = _ffn(xf, ln_ffn1[i], w1_gate[i], w1_up[i], w1_down[i])
        j = i // 2
        if i % 2 == 0:
            lambda_init = 0.8 - 0.6 * math.exp(-0.3 * i)
            wq, wk, wv = jnp.split(a_w_qkv[j], 3, axis=1)
            w = jnp.concatenate([_pair_halves_layout(wq), _pair_halves_layout(wk), wv],
                                axis=1).astype(BF16)
            q, k, v = _proj(xf, ln_mix[i], w, tables, (("q", (1,)), ("k", (1,)), ("v", (1,))),
                            s, "qkv_proj")
            o = _attn_a(q.reshape(b, s, d), k.reshape(b, s, d), v.reshape(b, s, d),
                        a_lambda[j], a_subln[j], lambda_init)
            xf = _out_proj(xf, o.reshape(n, d), a_w_o[j].astype(BF16))
        else:
            n_groups = len(B_GROUPS)
            parts = jnp.split(b_w_in[j], 2 * n_groups + 1, axis=1)
            w = jnp.concatenate([_pair_halves_layout(p) for p in parts[:-1]] + [parts[-1]],
                                axis=1).astype(BF16)
            dilations = tuple(dil for _, dil in B_GROUPS)
            chunks = sum(((("q", (dil,)), ("k", (dil,))) for dil in dilations), ())
            chunks += (("v", dilations),)
            proj = _proj(xf, ln_mix[i], w, tables, chunks, s, "dilated_proj")
            outs, lses = [], []
            for g, (window, dilation) in enumerate(B_GROUPS):
                o, lse = _attn_b_group(proj[2 * g], proj[2 * g + 1], proj[2 * n_groups + g],
                                       b, dilation, window // (2 * dilation))
                outs.append(o)
                lses.append(lse)
            xf = _mix_out(xf, outs, lses, dilations, b_w_o[j].astype(BF16))
        xf = _ffn(xf, ln_ffn2[i], w2_gate[i], w2_up[i], w2_down[i],
                  final_g=ln_final if i == depth - 1 else None)
    return xf.reshape(b, s, d)
```

```python
import functools
import math

import jax
import jax.numpy as jnp
from jax import lax
from jax.experimental import pallas as pl
from jax.experimental.pallas import tpu as pltpu

EPS = 1e-6
ROPE_THETA = 10000.0
NEG_INF = -1e30
HEAD_DIM = 64
ROT_HALF = HEAD_DIM // 2
LANES = 128
B_GROUPS = ((128, 1), (512, 4), (2048, 16))
B_BLOCK = 64

ROW_TILE = 512
FF_CHUNK = 256
A_Q_TILE = 128
A_KEY_CHUNK = 256
A_STEPS_PER_ITER = 6
B_CHAINS = 16
B_Q_TILE = 128
VMEM_LIMIT = 56 * 1024 * 1024

F32 = jnp.float32
BF16 = jnp.bfloat16


def _resident(shape):
    return pl.BlockSpec(shape, lambda *_: (0,) * len(shape), pipeline_mode=pl.Buffered(1))


def _rms_scale(x):
    return x * lax.rsqrt(jnp.mean(x * x, axis=-1, keepdims=True) + EPS)


def _ffn_kernel(x_ref, g_ref, wg_ref, wu_ref, wd_ref, *rest, n_chunks, final):
    if final:
        gf_ref, o_ref = rest
    else:
        (o_ref,) = rest
    x = x_ref[...]
    h = (_rms_scale(x) * g_ref[...]).astype(BF16)
    acc = jnp.zeros(x.shape, F32)
    for c in range(n_chunks):
        cols = slice(c * FF_CHUNK, (c + 1) * FF_CHUNK)
        gate = jnp.dot(h, wg_ref[:, cols].astype(BF16), preferred_element_type=F32)
        up = jnp.dot(h, wu_ref[:, cols].astype(BF16), preferred_element_type=F32)
        a = (gate * jax.nn.sigmoid(gate) * up).astype(BF16)
        acc = acc + jnp.dot(a, wd_ref[cols, :].astype(BF16), preferred_element_type=F32)
    y = x + 0.5 * acc
    if final:
        y = _rms_scale(y) * gf_ref[...]
    o_ref[...] = y


def _ffn(x, g, wg, wu, wd, final_g=None):
    n, d = x.shape
    ff = wg.shape[1]
    assert n % ROW_TILE == 0 and ff % FF_CHUNK == 0
    final = final_g is not None
    row = pl.BlockSpec((ROW_TILE, d), lambda i: (i, 0))
    in_specs = [row, _resident((1, d)), _resident((d, ff)), _resident((d, ff)), _resident((ff, d))]
    args = [x, g.reshape(1, d), wg, wu, wd]
    if final:
        in_specs.append(_resident((1, d)))
        args.append(final_g.reshape(1, d))
    return pl.pallas_call(
        functools.partial(_ffn_kernel, n_chunks=ff // FF_CHUNK, final=final),
        grid=(n // ROW_TILE,),
        in_specs=in_specs,
        out_specs=row,
        out_shape=jax.ShapeDtypeStruct((n, d), F32),
        compiler_params=pltpu.CompilerParams(
            dimension_semantics=("parallel",), vmem_limit_bytes=VMEM_LIMIT),
        name="ffn_final" if final else "ffn",
    )(*args)


STRIDED_LOAD_MAX = 4


def _phase_pitch(rows):
    return rows if (rows // 8) % 2 else rows + 8


def _proj_kernel(x_ref, g_ref, w_ref, cq_ref, sq_ref, ck_ref, sk_ref, *rest, chunks):
    out_refs, stage_ref, wide_ref = rest[:-2], rest[-2], rest[-1]
    x = x_ref[...]
    tm, d = x.shape
    h = (_rms_scale(x) * g_ref[...]).astype(BF16)
    o = 0
    for c, (kind, dilations) in enumerate(chunks):
        y = jnp.dot(h, w_ref[:, c * d:(c + 1) * d], preferred_element_type=F32)
        if kind != "v":
            cos = (cq_ref if kind == "q" else ck_ref)[...]
            sin = (sq_ref if kind == "q" else sk_ref)[...]
        for j in range(d // LANES):
            lanes = slice(j * LANES, (j + 1) * LANES)
            r = y[:, lanes]
            if kind != "v":
                r = r * cos + pltpu.roll(r, LANES // 2, 1) * sin
            if any(1 < dil <= STRIDED_LOAD_MAX for dil in dilations):
                stage_ref[j] = r
            for k, dil in enumerate(dilations):
                rows = tm // dil
                if dil == 1:
                    out_refs[o + k][:, lanes] = r.astype(BF16)
                    continue
                if dil <= STRIDED_LOAD_MAX:
                    for p in range(dil):
                        out_refs[o + k][:, p * d + j * LANES:p * d + (j + 1) * LANES] = (
                            stage_ref[j, pl.ds(p, rows, stride=dil), :].astype(BF16))
                    continue
                pitch = _phase_pitch(rows)
                for m0 in range(rows):
                    wide_ref[j, pl.ds(m0, dil, stride=pitch), :] = r[m0 * dil:(m0 + 1) * dil, :]
                for p in range(dil):
                    out_refs[o + k][:, p * d + j * LANES:p * d + (j + 1) * LANES] = (
                        wide_ref[j, p * pitch:p * pitch + rows, :].astype(BF16))
        o += len(dilations)


def _proj(x, g, w, tables, chunks, seq, name):
    n, d = x.shape
    assert n % ROW_TILE == 0 and seq % ROW_TILE == 0 and w.shape[1] == d * len(chunks)
    pos_blocks = seq // ROW_TILE
    row = pl.BlockSpec((ROW_TILE, d), lambda i: (i, 0))
    tab = pl.BlockSpec((ROW_TILE, LANES), lambda i: (i % pos_blocks, 0))
    dils = [dil for _, ds in chunks for dil in ds]
    wide_rows = max([dil * _phase_pitch(ROW_TILE // dil) for dil in dils if dil > STRIDED_LOAD_MAX],
                    default=8)
    return pl.pallas_call(
        functools.partial(_proj_kernel, chunks=chunks),
        grid=(n // ROW_TILE,),
        in_specs=[row, _resident((1, d)), _resident(w.shape), tab, tab, tab, tab],
        out_specs=[pl.BlockSpec((ROW_TILE // dil, dil * d), lambda i: (i, 0)) for dil in dils],
        out_shape=[jax.ShapeDtypeStruct((n // dil, dil * d), BF16) for dil in dils],
        scratch_shapes=[pltpu.VMEM((d // LANES, ROW_TILE, LANES), F32),
                        pltpu.VMEM((d // LANES, wide_rows, LANES), F32)],
        compiler_params=pltpu.CompilerParams(
            dimension_semantics=("parallel",), vmem_limit_bytes=VMEM_LIMIT),
        name=name,
    )(x, g.reshape(1, d), w, *tables)


def _rope_tables(seq):
    inv = ROPE_THETA ** (-jnp.arange(0, HEAD_DIM, 2, dtype=F32) / HEAD_DIM)
    ang = jnp.arange(seq, dtype=F32)[:, None] * inv[None, :]
    cos, sin = jnp.cos(ang), jnp.sin(ang)
    cos_t = jnp.concatenate([cos, cos, cos, cos], axis=1)
    sin_t = jnp.concatenate([-sin, -sin, sin, sin], axis=1)
    scale = HEAD_DIM ** -0.5 * math.log2(math.e)
    return cos_t * scale, sin_t * scale, cos_t, sin_t


def _pair_halves_layout(w_chunk):
    din, dout = w_chunk.shape
    w5 = w_chunk.reshape(din, dout // LANES, 2, 2, ROT_HALF)
    return w5.transpose(0, 1, 3, 2, 4).reshape(din, dout)


def _stack_pair(q):
    lane = lax.broadcasted_iota(jnp.int32, q.shape, 1)
    first = (lane % HEAD_DIM) < ROT_HALF
    zero = jnp.zeros_like(q)
    return jnp.concatenate([jnp.where(first, q, zero), jnp.where(first, zero, q)], axis=0)


def _attn_a_kernel(lam_ref, subln_ref, q_ref, k_ref, v_ref, o_ref,
                   s0_ref, s1_ref, m0_ref, m1_ref, acc0_ref, acc1_ref, vt_ref, *, lambda_init, tq):
    n_keys = k_ref.shape[1]
    n_tiles = q_ref.shape[1] // tq
    lam = lam_ref[...]
    lam_full = (jnp.exp(jnp.sum(lam[0:1] * lam[1:2], axis=-1, keepdims=True))
                - jnp.exp(jnp.sum(lam[2:3] * lam[3:4], axis=-1, keepdims=True)) + lambda_init)

    sub = 8
    n_chunks = n_keys // A_KEY_CHUNK

    bufs = ((s0_ref, m0_ref, acc0_ref), (s1_ref, m1_ref, acc1_ref))

    def step(i, parity, scores=True, numer=True, final=True):
        s_new, m_new, acc_fin = bufs[parity]
        s_old, m_old, acc_out = bufs[1 - parity]
        t_new = i if scores else None
        t_old = i - 1 if numer else None
        if final:
            saved = acc_fin[...]
            pv = saved[:LANES] / saved[LANES:LANES + 1]
            o = (pv[:, :tq] - lam_full * pv[:, tq:]).T
            o = _rms_scale(o) * subln_ref[...] * (1.0 - lambda_init)
            o_ref[0, pl.ds(pl.multiple_of((i - 2) * tq, tq), tq), :] = o.astype(BF16)
        if t_new is not None:
            q2t = _stack_pair(q_ref[0, pl.ds(pl.multiple_of(t_new * tq, tq), tq), :]).T
            top = jnp.full((sub, 2 * tq), -jnp.inf, F32)
        if t_old is not None:
            m = m_old[...]
            acc = jnp.zeros((vt_ref.shape[0], 2 * tq), F32)
        for kb in range(n_chunks):
            keys = slice(kb * A_KEY_CHUNK, (kb + 1) * A_KEY_CHUNK)
            if t_new is not None:
                st = jnp.dot(k_ref[0, keys, :], q2t, preferred_element_type=F32)
                s_new[keys, :] = st
                top = jnp.maximum(top, jnp.max(st.reshape(A_KEY_CHUNK // sub, sub, 2 * tq), axis=0))
            if t_old is not None:
                e = s_old[keys, :].reshape(A_KEY_CHUNK // sub, sub, 2 * tq) - m[None]
                pt = jnp.exp2(e).reshape(A_KEY_CHUNK, 2 * tq).astype(BF16)
                acc = acc + jnp.dot(vt_ref[:, keys], pt, preferred_element_type=F32)
        if t_new is not None:
            m_new[...] = jnp.broadcast_to(jnp.max(top, axis=0, keepdims=True), m_new.shape)
        if t_old is not None:
            acc_out[...] = acc

    vt_ref[:LANES, :] = v_ref[0].T
    vt_ref[LANES:, :] = jnp.ones((vt_ref.shape[0] - LANES, n_keys), BF16)

    step(0, 0, numer=False, final=False)
    step(1, 1, final=False)

    def steps(u, carry):
        for k in range(A_STEPS_PER_ITER):
            step(A_STEPS_PER_ITER * u + 2 + k, k % 2)
        return carry

    assert (n_tiles - 2) % A_STEPS_PER_ITER == 0 and A_STEPS_PER_ITER % 2 == 0
    lax.fori_loop(0, (n_tiles - 2) // A_STEPS_PER_ITER, steps, 0)
    step(n_tiles, 0, scores=False)
    step(n_tiles + 1, 1, scores=False, numer=False)


def _attn_a(q, k, v, lam, subln, lambda_init):
    b, s, d = q.shape
    heads = d // LANES
    tq = A_Q_TILE
    assert s % (2 * tq) == 0
    blk = pl.BlockSpec((1, s, LANES), lambda bi, hi: (bi, 0, hi))
    return pl.pallas_call(
        functools.partial(_attn_a_kernel, lambda_init=lambda_init, tq=tq),
        grid=(b, heads),
        in_specs=[_resident(lam.shape), _resident((1, LANES)), blk, blk, blk],
        out_specs=blk,
        out_shape=jax.ShapeDtypeStruct((b, s, d), BF16),
        scratch_shapes=[pltpu.VMEM((s, 2 * tq), F32), pltpu.VMEM((s, 2 * tq), F32),
                        pltpu.VMEM((8, 2 * tq), F32), pltpu.VMEM((8, 2 * tq), F32),
                        pltpu.VMEM((LANES + 16, 2 * tq), F32), pltpu.VMEM((LANES + 16, 2 * tq), F32),
                        pltpu.VMEM((LANES + 16, s), BF16)],
        compiler_params=pltpu.CompilerParams(
            dimension_semantics=("parallel", "parallel"), vmem_limit_bytes=VMEM_LIMIT),
        name="diff_attn",
    )(lam, subln.reshape(1, LANES), q, k, v)


def _out_kernel(x_ref, o_ref, w_ref, y_ref):
    y_ref[...] = x_ref[...] + jnp.dot(o_ref[...], w_ref[...].astype(BF16),
                                      preferred_element_type=F32)


def _out_proj(x, o, w):
    n, d = x.shape
    row = pl.BlockSpec((ROW_TILE, d), lambda i: (i, 0))
    return pl.pallas_call(
        _out_kernel,
        grid=(n // ROW_TILE,),
        in_specs=[row, row, _resident(w.shape)],
        out_specs=row,
        out_shape=jax.ShapeDtypeStruct((n, d), F32),
        compiler_params=pltpu.CompilerParams(
            dimension_semantics=("parallel",), vmem_limit_bytes=VMEM_LIMIT),
        name="out_proj",
    )(x, o, w)


def _attn_b_kernel(q_ref, k_ref, v_ref, o_ref, lse_ref, *, seq_len, half, blocks_per_phase):
    cw = q_ref.shape[2]
    tq = min(B_Q_TILE, seq_len)
    win = tq + 2 * half
    n_pairs = cw // LANES
    j = pl.program_id(1)
    first_pair = (j % blocks_per_phase) * n_pairs

    if blocks_per_phase > 1:
        @pl.when(j % blocks_per_phase == 0)
        def _():
            lse_ref[...] = jnp.zeros(lse_ref.shape, F32)

    lane = lax.broadcasted_iota(jnp.int32, (tq, LANES), 1)
    row = lax.broadcasted_iota(jnp.int32, (2 * tq, win), 0) % tq
    col = lax.broadcasted_iota(jnp.int32, (2 * tq, win), 1)
    ones = jnp.ones((win, LANES), BF16)

    def tile(t, carry):
        i0 = pl.multiple_of(t * tq, tq)
        w0 = pl.multiple_of(jnp.clip(i0 - half, 0, seq_len - win), half)
        band = jnp.where(jnp.abs((row + i0) - (col + w0)) <= half, 0.0, NEG_INF)
        if blocks_per_phase > 1:
            lse_tile = lse_ref[0, pl.ds(i0, tq), :]
        else:
            lse_tile = jnp.zeros((tq, LANES), F32)
        for pr in range(n_pairs):
            cols = slice(pr * LANES, (pr + 1) * LANES)
            lane_a = 2 * (first_pair + pr)
            q2 = _stack_pair(q_ref[0, pl.ds(i0, tq), cols])
            kw = k_ref[0, pl.ds(w0, win), cols]
            vw = v_ref[0, pl.ds(w0, win), cols]
            s = lax.dot_general(q2, kw, (((1,), (1,)), ((), ())),
                                preferred_element_type=F32)
            s = s + band
            m = jnp.max(s, axis=-1, keepdims=True)
            p = jnp.exp2(s - m)
            pvl = jnp.dot(p.astype(BF16), jnp.concatenate([vw, ones], axis=1),
                          preferred_element_type=F32)
            l = pvl[:, LANES:]
            pv = pvl[:, :LANES] / l
            o_ref[0, pl.ds(i0, tq), cols] = jnp.where(
                lane < HEAD_DIM, pv[:tq], pv[tq:]).astype(BF16)
            lse = m + jnp.log2(l)
            lse_tile = jnp.where(lane == lane_a, lse[:tq],
                                 jnp.where(lane == lane_a + 1, lse[tq:], lse_tile))
        lse_ref[0, pl.ds(i0, tq), :] = lse_tile
        return carry

    n_tiles = seq_len // tq
    lax.fori_loop(0, n_tiles, tile, 0, unroll=min(n_tiles, max(1, B_CHAINS // n_pairs)))


def _attn_b_group(q, k, v, b, dilation, half):
    d = q.shape[1] // dilation
    sl = q.shape[0] // b
    tq = min(B_Q_TILE, sl)
    assert sl % tq == 0 and sl >= tq + 2 * half
    cw = max(LANES, min(d, (1 << 20) // sl))
    assert d % cw == 0
    blocks_per_phase = d // cw
    view = lambda t: t.reshape(b, sl, dilation * d)
    blk = pl.BlockSpec((1, sl, cw), lambda bi, j: (bi, 0, j))
    lse_blk = pl.BlockSpec((1, sl, LANES), lambda bi, j: (bi, 0, j // blocks_per_phase))
    o, lse = pl.pallas_call(
        functools.partial(_attn_b_kernel, seq_len=sl, half=half, blocks_per_phase=blocks_per_phase),
        grid=(b, dilation * blocks_per_phase),
        in_specs=[blk, blk, blk],
        out_specs=[blk, lse_blk],
        out_shape=[jax.ShapeDtypeStruct((b, sl, dilation * d), BF16),
                   jax.ShapeDtypeStruct((b, sl, dilation * LANES), F32)],
        compiler_params=pltpu.CompilerParams(
            dimension_semantics=("parallel", "arbitrary"), vmem_limit_bytes=VMEM_LIMIT),
        name=f"dilated_attn_d{dilation}",
    )(view(q), view(k), view(v))
    return o.reshape(b * sl, dilation * d), lse.reshape(b * sl, dilation * LANES)


def _mix_out_kernel(x_ref, *rest, dilations):
    n_g = len(dilations)
    o_refs, l_refs = rest[:n_g], rest[n_g:2 * n_g]
    e_ref, w_ref, y_ref = rest[2 * n_g:2 * n_g + 3]
    stages = list(rest[2 * n_g + 3:])
    tm, d = x_ref.shape
    n_lanes = d // LANES

    lses, o_stage = [], []
    for o_ref, l_ref, dil in zip(o_refs, l_refs, dilations):
        if dil == 1:
            lses.append(l_ref[...])
            o_stage.append(None)
            continue
        ost, lst = stages.pop(0), stages.pop(0)
        for p in range(dil):
            rows = pl.ds(p, tm // dil, stride=dil)
            lst[rows, :] = l_ref[:, p * LANES:(p + 1) * LANES]
            for j in range(n_lanes):
                ost[j, rows, :] = o_ref[:, p * d + j * LANES:p * d + (j + 1) * LANES].astype(F32)
        lses.append(lst[...])
        o_stage.append(ost)

    top = functools.reduce(jnp.maximum, lses)
    ws = [jnp.exp2(t - top) for t in lses]
    inv = 1.0 / functools.reduce(jnp.add, ws)
    e = e_ref[...]
    spreads = []
    for wgt in ws:
        alpha = wgt * inv
        hi = alpha.astype(BF16)
        lo = (alpha - hi.astype(F32)).astype(BF16)
        spreads.append(jnp.dot(jnp.concatenate([hi, lo], axis=1), e, preferred_element_type=F32))
    parts = []
    for j in range(n_lanes):
        lanes = slice(j * LANES, (j + 1) * LANES)
        acc = jnp.zeros((tm, LANES), F32)
        for o_ref, ost, spread in zip(o_refs, o_stage, spreads):
            og = o_ref[:, lanes].astype(F32) if ost is None else ost[j]
            acc = acc + spread[:, lanes] * og
        parts.append(acc.astype(BF16))
    y_ref[...] = x_ref[...] + jnp.dot(jnp.concatenate(parts, axis=1), w_ref[...].astype(BF16),
                                      preferred_element_type=F32)


def _mix_out(x, outs, lses, dilations, w):
    n, d = x.shape
    assert d // HEAD_DIM <= LANES
    spread = (jnp.arange(2 * LANES)[:, None] % LANES
              == (jnp.arange(d)[None, :] // HEAD_DIM)).astype(BF16)
    row = pl.BlockSpec((ROW_TILE, d), lambda i: (i, 0))
    o_specs = [pl.BlockSpec((ROW_TILE // dil, dil * d), lambda i: (i, 0)) for dil in dilations]
    l_specs = [pl.BlockSpec((ROW_TILE // dil, dil * LANES), lambda i: (i, 0)) for dil in dilations]
    scratch = []
    for dil in dilations:
        if dil > 1:
            scratch += [pltpu.VMEM((d // LANES, ROW_TILE, LANES), F32),
                        pltpu.VMEM((ROW_TILE, LANES), F32)]
    return pl.pallas_call(
        functools.partial(_mix_out_kernel, dilations=tuple(dilations)),
        grid=(n // ROW_TILE,),
        in_specs=[row, *o_specs, *l_specs, _resident(spread.shape), _resident(w.shape)],
        out_specs=row,
        out_shape=jax.ShapeDtypeStruct((n, d), F32),
        scratch_shapes=scratch,
        compiler_params=pltpu.CompilerParams(
            dimension_semantics=("parallel",), vmem_limit_bytes=VMEM_LIMIT),
        name="mix_out_proj",
    )(x, *outs, *lses, spread, w)


def kernel(x, ln_ffn1, w1_gate, w1_up, w1_down, ln_mix, a_w_qkv, a_w_o, a_lambda, a_subln,
           b_w_in, b_w_o, ln_ffn2, w2_gate, w2_up, w2_down, ln_final):
    b, s, d = x.shape
    depth = ln_ffn1.shape[0]
    n = b * s
    tables = _rope_tables(s)
    xf = x.reshape(n, d)
    for i in range(depth):
        xf = _ffn(xf, ln_ffn1[i], w1_gate[i], w1_up[i], w1_down[i])
        j = i // 2
        if i % 2 == 0:
            lambda_init = 0.8 - 0.6 * math.exp(-0.3 * i)
            wq, wk, wv = jnp.split(a_w_qkv[j], 3, axis=1)
            w = jnp.concatenate([_pair_halves_layout(wq), _pair_halves_layout(wk), wv],
                                axis=1).astype(BF16)
            q, k, v = _proj(xf, ln_mix[i], w, tables, (("q", (1,)), ("k", (1,)), ("v", (1,))),
                            s, "qkv_proj")
            o = _attn_a(q.reshape(b, s, d), k.reshape(b, s, d), v.reshape(b, s, d),
                        a_lambda[j], a_subln[j], lambda_init)
            xf = _out_proj(xf, o.reshape(n, d), a_w_o[j])
        else:
            n_groups = len(B_GROUPS)
            parts = jnp.split(b_w_in[j], 2 * n_groups + 1, axis=1)
            w = jnp.concatenate([_pair_halves_layout(p) for p in parts[:-1]] + [parts[-1]],
                                axis=1).astype(BF16)
            dilations = tuple(dil for _, dil in B_GROUPS)
            chunks = sum(((("q", (dil,)), ("k", (dil,))) for dil in dilations), ())
            chunks += (("v", dilations),)
            proj = _proj(xf, ln_mix[i], w, tables, chunks, s, "dilated_proj")
            outs, lses = [], []
            for g, (window, dilation) in enumerate(B_GROUPS):
                o, lse = _attn_b_group(proj[2 * g], proj[2 * g + 1], proj[2 * n_groups + g],
                                       b, dilation, window // (2 * dilation))
                outs.append(o)
                lses.append(lse)
            xf = _mix_out(xf, outs, lses, dilations, b_w_o[j])
        xf = _ffn(xf, ln_ffn2[i], w2_gate[i], w2_up[i], w2_down[i],
                  final_g=ln_final if i == depth - 1 else None)
    return xf.reshape(b, s, d)
```

```python
import functools
import math

import jax
import jax.numpy as jnp
from jax import lax
from jax.experimental import pallas as pl
from jax.experimental.pallas import tpu as pltpu

EPS = 1e-6
ROPE_THETA = 10000.0
NEG_INF = -1e30
HEAD_DIM = 64
ROT_HALF = HEAD_DIM // 2
LANES = 128
B_GROUPS = ((128, 1), (512, 4), (2048, 16))
B_BLOCK = 64

ROW_TILE = 512
FF_CHUNK = 256
A_Q_TILE = 128
A_KEY_CHUNK = 256
A_STEPS_PER_ITER = 6
B_CHAINS = 16
B_Q_TILE = 128
VMEM_LIMIT = 56 * 1024 * 1024

F32 = jnp.float32
BF16 = jnp.bfloat16


def _resident(shape):
    return pl.BlockSpec(shape, lambda *_: (0,) * len(shape), pipeline_mode=pl.Buffered(1))


def _rms_scale(x):
    return x * lax.rsqrt(jnp.mean(x * x, axis=-1, keepdims=True) + EPS)


def _ffn_kernel(x_ref, g_ref, wg_ref, wu_ref, wd_ref, *rest, n_chunks, mixer, final):
    rest = list(rest)
    o_ref = rest.pop()
    x = x_ref[...]
    if mixer:
        a_ref, wo_ref = rest.pop(0), rest.pop(0)
        x = x + jnp.dot(a_ref[...], wo_ref[...].astype(BF16), preferred_element_type=F32)
    if final:
        gf_ref = rest.pop(0)
    h = (_rms_scale(x) * g_ref[...]).astype(BF16)
    acc = jnp.zeros(x.shape, F32)
    for c in range(n_chunks):
        cols = slice(c * FF_CHUNK, (c + 1) * FF_CHUNK)
        gate = jnp.dot(h, wg_ref[:, cols].astype(BF16), preferred_element_type=F32)
        up = jnp.dot(h, wu_ref[:, cols].astype(BF16), preferred_element_type=F32)
        a = (gate * jax.nn.sigmoid(gate) * up).astype(BF16)
        acc = acc + jnp.dot(a, wd_ref[cols, :].astype(BF16), preferred_element_type=F32)
    y = x + 0.5 * acc
    if final:
        y = _rms_scale(y) * gf_ref[...]
    o_ref[...] = y


def _layer(shape, layer):
    return pl.BlockSpec((None, *shape), lambda *_: (layer,) + (0,) * len(shape),
                        pipeline_mode=pl.Buffered(1))


def _ffn(x, g, wg, wu, wd, layer, mixer=None, final_g=None):
    n, d = x.shape
    ff = wg.shape[2]
    assert n % ROW_TILE == 0 and ff % FF_CHUNK == 0
    final = final_g is not None
    row = pl.BlockSpec((ROW_TILE, d), lambda i: (i, 0))
    in_specs = [row, _resident((1, d)), _layer((d, ff), layer), _layer((d, ff), layer),
                _layer((ff, d), layer)]
    args = [x, g.reshape(1, d), wg, wu, wd]
    if mixer is not None:
        in_specs += [row, _resident((d, d))]
        args += list(mixer)
    if final:
        in_specs.append(_resident((1, d)))
        args.append(final_g.reshape(1, d))
    return pl.pallas_call(
        functools.partial(_ffn_kernel, n_chunks=ff // FF_CHUNK, mixer=mixer is not None,
                          final=final),
        grid=(n // ROW_TILE,),
        in_specs=in_specs,
        out_specs=row,
        out_shape=jax.ShapeDtypeStruct((n, d), F32),
        compiler_params=pltpu.CompilerParams(
            dimension_semantics=("parallel",), vmem_limit_bytes=VMEM_LIMIT),
        name="ffn" + ("_mixer" if mixer is not None else "") + ("_final" if final else ""),
    )(*args)


STRIDED_LOAD_MAX = 4


def _phase_pitch(rows):
    return rows if (rows // 8) % 2 else rows + 8


def _proj_kernel(x_ref, g_ref, w_ref, tq_ref, tk_ref, *rest, chunks):
    out_refs, stage_ref, wide_ref = rest[:-2], rest[-2], rest[-1]
    x = x_ref[...]
    tm, d = x.shape
    h = (_rms_scale(x) * g_ref[...]).astype(BF16)
    o = 0
    for c, (kind, dilations) in enumerate(chunks):
        y = jnp.dot(h, w_ref[:, c * d:(c + 1) * d].astype(BF16), preferred_element_type=F32)
        if kind != "v":
            t_ref = tq_ref if kind == "q" else tk_ref
            cos, sin_up, sin_dn = t_ref[0], t_ref[1], t_ref[2]
        for j in range(d // LANES):
            lanes = slice(j * LANES, (j + 1) * LANES)
            r = y[:, lanes]
            if kind != "v":
                r = (r * cos + pltpu.roll(r, LANES - ROT_HALF, 1) * sin_up
                     + pltpu.roll(r, ROT_HALF, 1) * sin_dn)
            if any(1 < dil <= STRIDED_LOAD_MAX for dil in dilations):
                stage_ref[j] = r
            for k, dil in enumerate(dilations):
                rows = tm // dil
                if dil == 1:
                    out_refs[o + k][:, lanes] = r.astype(BF16)
                    continue
                if dil <= STRIDED_LOAD_MAX:
                    for p in range(dil):
                        out_refs[o + k][:, p * d + j * LANES:p * d + (j + 1) * LANES] = (
                            stage_ref[j, pl.ds(p, rows, stride=dil), :].astype(BF16))
                    continue
                pitch = _phase_pitch(rows)
                for m0 in range(rows):
                    wide_ref[j, pl.ds(m0, dil, stride=pitch), :] = r[m0 * dil:(m0 + 1) * dil, :]
                for p in range(dil):
                    out_refs[o + k][:, p * d + j * LANES:p * d + (j + 1) * LANES] = (
                        wide_ref[j, p * pitch:p * pitch + rows, :].astype(BF16))
        o += len(dilations)


def _proj(x, g, w, tables, chunks, seq, name):
    n, d = x.shape
    assert n % ROW_TILE == 0 and seq % ROW_TILE == 0 and w.shape[1] == d * len(chunks)
    pos_blocks = seq // ROW_TILE
    row = pl.BlockSpec((ROW_TILE, d), lambda i: (i, 0))
    tab = pl.BlockSpec((3, ROW_TILE, LANES), lambda i: (0, i % pos_blocks, 0))
    dils = [dil for _, ds in chunks for dil in ds]
    wide_rows = max([dil * _phase_pitch(ROW_TILE // dil) for dil in dils if dil > STRIDED_LOAD_MAX],
                    default=8)
    return pl.pallas_call(
        functools.partial(_proj_kernel, chunks=chunks),
        grid=(n // ROW_TILE,),
        in_specs=[row, _resident((1, d)), _resident(w.shape), tab, tab],
        out_specs=[pl.BlockSpec((ROW_TILE // dil, dil * d), lambda i: (i, 0)) for dil in dils],
        out_shape=[jax.ShapeDtypeStruct((n // dil, dil * d), BF16) for dil in dils],
        scratch_shapes=[pltpu.VMEM((d // LANES, ROW_TILE, LANES), F32),
                        pltpu.VMEM((d // LANES, wide_rows, LANES), F32)],
        compiler_params=pltpu.CompilerParams(
            dimension_semantics=("parallel",), vmem_limit_bytes=VMEM_LIMIT),
        name=name,
    )(x, g.reshape(1, d), w, *tables)


def _rope_tables(seq):
    inv = ROPE_THETA ** (-jnp.arange(0, HEAD_DIM, 2, dtype=F32) / HEAD_DIM)
    ang = jnp.arange(seq, dtype=F32)[:, None] * inv[None, :]
    cos, sin = jnp.cos(ang), jnp.sin(ang)
    zero = jnp.zeros_like(sin)
    tables = jnp.stack([jnp.concatenate([cos, cos, cos, cos], axis=1),
                        jnp.concatenate([-sin, zero, -sin, zero], axis=1),
                        jnp.concatenate([zero, sin, zero, sin], axis=1)])
    scale = HEAD_DIM ** -0.5 * math.log2(math.e)
    return tables * scale, tables


def _stack_pair(q):
    lane = lax.broadcasted_iota(jnp.int32, q.shape, 1)
    first = lane < HEAD_DIM
    zero = jnp.zeros_like(q)
    return jnp.concatenate([jnp.where(first, q, zero), jnp.where(first, zero, q)], axis=0)


def _attn_a_kernel(lam_ref, subln_ref, q_ref, k_ref, v_ref, o_ref,
                   s0_ref, s1_ref, m0_ref, m1_ref, acc0_ref, acc1_ref, vt_ref, *, lambda_init, tq):
    n_keys = k_ref.shape[1]
    n_tiles = q_ref.shape[1] // tq
    lam = lam_ref[...]
    lam_full = (jnp.exp(jnp.sum(lam[0:1] * lam[1:2], axis=-1, keepdims=True))
                - jnp.exp(jnp.sum(lam[2:3] * lam[3:4], axis=-1, keepdims=True)) + lambda_init)

    sub = 8
    n_chunks = n_keys // A_KEY_CHUNK

    bufs = ((s0_ref, m0_ref, acc0_ref), (s1_ref, m1_ref, acc1_ref))

    def step(i, parity, scores=True, numer=True, final=True):
        s_new, m_new, acc_fin = bufs[parity]
        s_old, m_old, acc_out = bufs[1 - parity]
        t_new = i if scores else None
        t_old = i - 1 if numer else None
        if final:
            saved = acc_fin[...]
            pv = saved[:LANES] / saved[LANES:LANES + 1]
            o = (pv[:, :tq] - lam_full * pv[:, tq:]).T
            o = _rms_scale(o) * subln_ref[...] * (1.0 - lambda_init)
            o_ref[0, pl.ds(pl.multiple_of((i - 2) * tq, tq), tq), :] = o.astype(BF16)
        if t_new is not None:
            q2t = _stack_pair(q_ref[0, pl.ds(pl.multiple_of(t_new * tq, tq), tq), :]).T
            top = jnp.full((sub, 2 * tq), -jnp.inf, F32)
        if t_old is not None:
            m = m_old[...]
            acc = jnp.zeros((vt_ref.shape[0], 2 * tq), F32)
        for kb in range(n_chunks):
            keys = slice(kb * A_KEY_CHUNK, (kb + 1) * A_KEY_CHUNK)
            if t_new is not None:
                st = jnp.dot(k_ref[0, keys, :], q2t, preferred_element_type=F32)
                s_new[keys, :] = st
                top = jnp.maximum(top, jnp.max(st.reshape(A_KEY_CHUNK // sub, sub, 2 * tq), axis=0))
            if t_old is not None:
                e = s_old[keys, :].reshape(A_KEY_CHUNK // sub, sub, 2 * tq) - m[None]
                pt = jnp.exp2(e).reshape(A_KEY_CHUNK, 2 * tq).astype(BF16)
                acc = acc + jnp.dot(vt_ref[:, keys], pt, preferred_element_type=F32)
        if t_new is not None:
            m_new[...] = jnp.broadcast_to(jnp.max(top, axis=0, keepdims=True), m_new.shape)
        if t_old is not None:
            acc_out[...] = acc

    vt_ref[:LANES, :] = v_ref[0].T
    vt_ref[LANES:, :] = jnp.ones((vt_ref.shape[0] - LANES, n_keys), BF16)

    step(0, 0, numer=False, final=False)
    step(1, 1, final=False)

    def steps(u, carry):
        for k in range(A_STEPS_PER_ITER):
            step(A_STEPS_PER_ITER * u + 2 + k, k % 2)
        return carry

    assert (n_tiles - 2) % A_STEPS_PER_ITER == 0 and A_STEPS_PER_ITER % 2 == 0
    lax.fori_loop(0, (n_tiles - 2) // A_STEPS_PER_ITER, steps, 0)
    step(n_tiles, 0, scores=False)
    step(n_tiles + 1, 1, scores=False, numer=False)


def _attn_a(q, k, v, lam, subln, lambda_init):
    b, s, d = q.shape
    heads = d // LANES
    tq = A_Q_TILE
    assert s % (2 * tq) == 0
    blk = pl.BlockSpec((1, s, LANES), lambda bi, hi: (bi, 0, hi))
    return pl.pallas_call(
        functools.partial(_attn_a_kernel, lambda_init=lambda_init, tq=tq),
        grid=(b, heads),
        in_specs=[_resident(lam.shape), _resident((1, LANES)), blk, blk, blk],
        out_specs=blk,
        out_shape=jax.ShapeDtypeStruct((b, s, d), BF16),
        scratch_shapes=[pltpu.VMEM((s, 2 * tq), F32), pltpu.VMEM((s, 2 * tq), F32),
                        pltpu.VMEM((8, 2 * tq), F32), pltpu.VMEM((8, 2 * tq), F32),
                        pltpu.VMEM((LANES + 16, 2 * tq), F32), pltpu.VMEM((LANES + 16, 2 * tq), F32),
                        pltpu.VMEM((LANES + 16, s), BF16)],
        compiler_params=pltpu.CompilerParams(
            dimension_semantics=("parallel", "parallel"), vmem_limit_bytes=VMEM_LIMIT),
        name="diff_attn",
    )(lam, subln.reshape(1, LANES), q, k, v)


def _attn_b_kernel(q_ref, k_ref, v_ref, o_ref, lse_ref, *, seq_len, half, blocks_per_phase):
    cw = q_ref.shape[2]
    tq = min(B_Q_TILE, seq_len)
    win = tq + 2 * half
    n_pairs = cw // LANES
    j = pl.program_id(1)
    first_pair = (j % blocks_per_phase) * n_pairs

    if blocks_per_phase > 1:
        @pl.when(j % blocks_per_phase == 0)
        def _():
            lse_ref[...] = jnp.zeros(lse_ref.shape, F32)

    lane = lax.broadcasted_iota(jnp.int32, (tq, LANES), 1)
    row = lax.broadcasted_iota(jnp.int32, (2 * tq, win), 0) % tq
    col = lax.broadcasted_iota(jnp.int32, (2 * tq, win), 1)
    ones = jnp.ones((win, LANES), BF16)

    def tile(t, carry):
        i0 = pl.multiple_of(t * tq, tq)
        w0 = pl.multiple_of(jnp.clip(i0 - half, 0, seq_len - win), half)
        band = jnp.where(jnp.abs((row + i0) - (col + w0)) <= half, 0.0, NEG_INF)
        if blocks_per_phase > 1:
            lse_tile = lse_ref[0, pl.ds(i0, tq), :]
        else:
            lse_tile = jnp.zeros((tq, LANES), F32)
        for pr in range(n_pairs):
            cols = slice(pr * LANES, (pr + 1) * LANES)
            lane_a = 2 * (first_pair + pr)
            q2 = _stack_pair(q_ref[0, pl.ds(i0, tq), cols])
            kw = k_ref[0, pl.ds(w0, win), cols]
            vw = v_ref[0, pl.ds(w0, win), cols]
            s = lax.dot_general(q2, kw, (((1,), (1,)), ((), ())),
                                preferred_element_type=F32)
            s = s + band
            m = jnp.max(s, axis=-1, keepdims=True)
            p = jnp.exp2(s - m)
            pvl = jnp.dot(p.astype(BF16), jnp.concatenate([vw, ones], axis=1),
                          preferred_element_type=F32)
            l = pvl[:, LANES:]
            pv = pvl[:, :LANES] / l
            o_ref[0, pl.ds(i0, tq), cols] = jnp.where(
                lane < HEAD_DIM, pv[:tq], pv[tq:]).astype(BF16)
            lse = m + jnp.log2(l)
            lse_tile = jnp.where(lane == lane_a, lse[:tq],
                                 jnp.where(lane == lane_a + 1, lse[tq:], lse_tile))
        lse_ref[0, pl.ds(i0, tq), :] = lse_tile
        return carry

    n_tiles = seq_len // tq
    lax.fori_loop(0, n_tiles, tile, 0, unroll=min(n_tiles, max(1, B_CHAINS // n_pairs)))


def _attn_b_group(q, k, v, b, dilation, half):
    d = q.shape[1] // dilation
    sl = q.shape[0] // b
    tq = min(B_Q_TILE, sl)
    assert sl % tq == 0 and sl >= tq + 2 * half
    cw = max(LANES, min(d, (1 << 20) // sl))
    assert d % cw == 0
    blocks_per_phase = d // cw
    view = lambda t: t.reshape(b, sl, dilation * d)
    blk = pl.BlockSpec((1, sl, cw), lambda bi, j: (bi, 0, j))
    lse_blk = pl.BlockSpec((1, sl, LANES), lambda bi, j: (bi, 0, j // blocks_per_phase))
    o, lse = pl.pallas_call(
        functools.partial(_attn_b_kernel, seq_len=sl, half=half, blocks_per_phase=blocks_per_phase),
        grid=(b, dilation * blocks_per_phase),
        in_specs=[blk, blk, blk],
        out_specs=[blk, lse_blk],
        out_shape=[jax.ShapeDtypeStruct((b, sl, dilation * d), BF16),
                   jax.ShapeDtypeStruct((b, sl, dilation * LANES), F32)],
        compiler_params=pltpu.CompilerParams(
            dimension_semantics=("parallel", "arbitrary"), vmem_limit_bytes=VMEM_LIMIT),
        name=f"dilated_attn_d{dilation}",
    )(view(q), view(k), view(v))
    return o.reshape(b * sl, dilation * d), lse.reshape(b * sl, dilation * LANES)


def _mix_out_kernel(x_ref, *rest, dilations):
    n_g = len(dilations)
    o_refs, l_refs = rest[:n_g], rest[n_g:2 * n_g]
    e_ref, w_ref, y_ref = rest[2 * n_g:2 * n_g + 3]
    stages = list(rest[2 * n_g + 3:])
    tm, d = x_ref.shape
    n_lanes = d // LANES

    lses, o_stage = [], []
    for o_ref, l_ref, dil in zip(o_refs, l_refs, dilations):
        if dil == 1:
            lses.append(l_ref[...])
            o_stage.append(None)
            continue
        ost, lst = stages.pop(0), stages.pop(0)
        for p in range(dil):
            rows = pl.ds(p, tm // dil, stride=dil)
            lst[rows, :] = l_ref[:, p * LANES:(p + 1) * LANES]
            for j in range(n_lanes):
                ost[j, rows, :] = o_ref[:, p * d + j * LANES:p * d + (j + 1) * LANES].astype(F32)
        lses.append(lst[...])
        o_stage.append(ost)

    top = functools.reduce(jnp.maximum, lses)
    ws = [jnp.exp2(t - top) for t in lses]
    inv = 1.0 / functools.reduce(jnp.add, ws)
    e = e_ref[...]
    spreads = []
    for wgt in ws:
        alpha = wgt * inv
        hi = alpha.astype(BF16)
        lo = (alpha - hi.astype(F32)).astype(BF16)
        spreads.append(jnp.dot(jnp.concatenate([hi, lo], axis=1), e, preferred_element_type=F32))
    parts = []
    for j in range(n_lanes):
        lanes = slice(j * LANES, (j + 1) * LANES)
        acc = jnp.zeros((tm, LANES), F32)
        for o_ref, ost, spread in zip(o_refs, o_stage, spreads):
            og = o_ref[:, lanes].astype(F32) if ost is None else ost[j]
            acc = acc + spread[:, lanes] * og
        parts.append(acc.astype(BF16))
    y_ref[...] = x_ref[...] + jnp.dot(jnp.concatenate(parts, axis=1), w_ref[...].astype(BF16),
                                      preferred_element_type=F32)


def _mix_out(x, outs, lses, dilations, w):
    n, d = x.shape
    assert d // HEAD_DIM <= LANES
    spread = (jnp.arange(2 * LANES)[:, None] % LANES
              == (jnp.arange(d)[None, :] // HEAD_DIM)).astype(BF16)
    row = pl.BlockSpec((ROW_TILE, d), lambda i: (i, 0))
    o_specs = [pl.BlockSpec((ROW_TILE // dil, dil * d), lambda i: (i, 0)) for dil in dilations]
    l_specs = [pl.BlockSpec((ROW_TILE // dil, dil * LANES), lambda i: (i, 0)) for dil in dilations]
    scratch = []
    for dil in dilations:
        if dil > 1:
            scratch += [pltpu.VMEM((d // LANES, ROW_TILE, LANES), F32),
                        pltpu.VMEM((ROW_TILE, LANES), F32)]
    return pl.pallas_call(
        functools.partial(_mix_out_kernel, dilations=tuple(dilations)),
        grid=(n // ROW_TILE,),
        in_specs=[row, *o_specs, *l_specs, _resident(spread.shape), _resident(w.shape)],
        out_specs=row,
        out_shape=jax.ShapeDtypeStruct((n, d), F32),
        scratch_shapes=scratch,
        compiler_params=pltpu.CompilerParams(
            dimension_semantics=("parallel",), vmem_limit_bytes=VMEM_LIMIT),
        name="mix_out_proj",
    )(x, *outs, *lses, spread, w)


def kernel(x, ln_ffn1, w1_gate, w1_up, w1_down, ln_mix, a_w_qkv, a_w_o, a_lambda, a_subln,
           b_w_in, b_w_o, ln_ffn2, w2_gate, w2_up, w2_down, ln_final):
    b, s, d = x.shape
    depth = ln_ffn1.shape[0]
    n = b * s
    tables = _rope_tables(s)
    xf = x.reshape(n, d)
    for i in range(depth):
        xf = _ffn(xf, ln_ffn1[i], w1_gate, w1_up, w1_down, i)
        j = i // 2
        if i % 2 == 0:
            lambda_init = 0.8 - 0.6 * math.exp(-0.3 * i)
            q, k, v = _proj(xf, ln_mix[i], a_w_qkv[j], tables, (("q", (1,)), ("k", (1,)), ("v", (1,))),
                            s, "qkv_proj")
            o = _attn_a(q.reshape(b, s, d), k.reshape(b, s, d), v.reshape(b, s, d),
                        a_lambda[j], a_subln[j], lambda_init)
            mixer = (o.reshape(n, d), a_w_o[j])
        else:
            n_groups = len(B_GROUPS)
            w = b_w_in[j].astype(BF16)
            dilations = tuple(dil for _, dil in B_GROUPS)
            chunks = sum(((("q", (dil,)), ("k", (dil,))) for dil in dilations), ())
            chunks += (("v", dilations),)
            proj = _proj(xf, ln_mix[i], w, tables, chunks, s, "dilated_proj")
            outs, lses = [], []
            for g, (window, dilation) in enumerate(B_GROUPS):
                o, lse = _attn_b_group(proj[2 * g], proj[2 * g + 1], proj[2 * n_groups + g],
                                       b, dilation, window // (2 * dilation))
                outs.append(o)
                lses.append(lse)
            xf = _mix_out(xf, outs, lses, dilations, b_w_o[j])
            mixer = None
        xf = _ffn(xf, ln_ffn2[i], w2_gate, w2_up, w2_down, i, mixer=mixer,
                  final_g=ln_final if i == depth - 1 else None)
    return xf.reshape(b, s, d)
```

```python
import functools
import math

import jax
import jax.numpy as jnp
from jax import lax
from jax.experimental import pallas as pl
from jax.experimental.pallas import tpu as pltpu

EPS = 1e-6
ROPE_THETA = 10000.0
NEG_INF = -1e30
HEAD_DIM = 64
ROT_HALF = HEAD_DIM // 2
LANES = 128
B_GROUPS = ((128, 1), (512, 4), (2048, 16))
B_BLOCK = 64

ROW_TILE = 512
FF_CHUNK = 256
A_Q_TILE = 128
A_KEY_CHUNK = 256
A_STEPS_PER_ITER = 6
B_SUM_LANE = 16
B_CHAINS = 16
B_Q_TILE = 128
VMEM_LIMIT = 56 * 1024 * 1024

F32 = jnp.float32
BF16 = jnp.bfloat16


def _resident(shape):
    return pl.BlockSpec(shape, lambda *_: (0,) * len(shape), pipeline_mode=pl.Buffered(1))


def _rms_scale(x):
    return x * lax.rsqrt(jnp.mean(x * x, axis=-1, keepdims=True) + EPS)


def _ffn_kernel(x_ref, g_ref, wg_ref, wu_ref, wd_ref, *rest, n_chunks, mixer, final):
    rest = list(rest)
    o_ref = rest.pop()
    x = x_ref[...]
    if mixer:
        a_ref, wo_ref = rest.pop(0), rest.pop(0)
        x = x + jnp.dot(a_ref[...], wo_ref[...].astype(BF16), preferred_element_type=F32)
    if final:
        gf_ref = rest.pop(0)
    h = (_rms_scale(x) * g_ref[...]).astype(BF16)
    acc = jnp.zeros(x.shape, F32)
    for c in range(n_chunks):
        cols = slice(c * FF_CHUNK, (c + 1) * FF_CHUNK)
        gate = jnp.dot(h, wg_ref[:, cols].astype(BF16), preferred_element_type=F32)
        up = jnp.dot(h, wu_ref[:, cols].astype(BF16), preferred_element_type=F32)
        a = (gate * jax.nn.sigmoid(gate) * up).astype(BF16)
        acc = acc + jnp.dot(a, wd_ref[cols, :].astype(BF16), preferred_element_type=F32)
    y = x + 0.5 * acc
    if final:
        y = _rms_scale(y) * gf_ref[...]
    o_ref[...] = y


def _layer(shape, layer):
    return pl.BlockSpec((None, *shape), lambda *_: (layer,) + (0,) * len(shape),
                        pipeline_mode=pl.Buffered(1))


def _ffn(x, g, wg, wu, wd, layer, mixer=None, final_g=None):
    n, d = x.shape
    ff = wg.shape[2]
    assert n % ROW_TILE == 0 and ff % FF_CHUNK == 0
    final = final_g is not None
    row = pl.BlockSpec((ROW_TILE, d), lambda i: (i, 0))
    in_specs = [row, _resident((1, d)), _layer((d, ff), layer), _layer((d, ff), layer),
                _layer((ff, d), layer)]
    args = [x, g.reshape(1, d), wg, wu, wd]
    if mixer is not None:
        in_specs += [row, _resident((d, d))]
        args += list(mixer)
    if final:
        in_specs.append(_resident((1, d)))
        args.append(final_g.reshape(1, d))
    return pl.pallas_call(
        functools.partial(_ffn_kernel, n_chunks=ff // FF_CHUNK, mixer=mixer is not None,
                          final=final),
        grid=(n // ROW_TILE,),
        in_specs=in_specs,
        out_specs=row,
        out_shape=jax.ShapeDtypeStruct((n, d), F32),
        compiler_params=pltpu.CompilerParams(
            dimension_semantics=("parallel",), vmem_limit_bytes=VMEM_LIMIT),
        name="ffn" + ("_mixer" if mixer is not None else "") + ("_final" if final else ""),
    )(*args)


STRIDED_LOAD_MAX = 4


def _phase_pitch(rows):
    return rows if (rows // 8) % 2 else rows + 8


def _proj_kernel(x_ref, g_ref, w_ref, tq_ref, tk_ref, *rest, chunks):
    out_refs, stage_ref, wide_ref = rest[:-2], rest[-2], rest[-1]
    x = x_ref[...]
    tm, d = x.shape
    h = (_rms_scale(x) * g_ref[...]).astype(BF16)
    o = 0
    for c, (kind, dilations) in enumerate(chunks):
        y = jnp.dot(h, w_ref[:, c * d:(c + 1) * d].astype(BF16), preferred_element_type=F32)
        if kind != "v":
            t_ref = tq_ref if kind == "q" else tk_ref
            cos, sin_up, sin_dn = t_ref[0], t_ref[1], t_ref[2]
        for j in range(d // LANES):
            lanes = slice(j * LANES, (j + 1) * LANES)
            r = y[:, lanes]
            if kind != "v":
                r = (r * cos + pltpu.roll(r, LANES - ROT_HALF, 1) * sin_up
                     + pltpu.roll(r, ROT_HALF, 1) * sin_dn)
            if any(1 < dil <= STRIDED_LOAD_MAX for dil in dilations):
                stage_ref[j] = r
            for k, dil in enumerate(dilations):
                rows = tm // dil
                if dil == 1:
                    out_refs[o + k][:, lanes] = r.astype(BF16)
                    continue
                if dil <= STRIDED_LOAD_MAX:
                    for p in range(dil):
                        out_refs[o + k][:, p * d + j * LANES:p * d + (j + 1) * LANES] = (
                            stage_ref[j, pl.ds(p, rows, stride=dil), :].astype(BF16))
                    continue
                pitch = _phase_pitch(rows)
                for m0 in range(rows):
                    wide_ref[j, pl.ds(m0, dil, stride=pitch), :] = r[m0 * dil:(m0 + 1) * dil, :]
                for p in range(dil):
                    out_refs[o + k][:, p * d + j * LANES:p * d + (j + 1) * LANES] = (
                        wide_ref[j, p * pitch:p * pitch + rows, :].astype(BF16))
        o += len(dilations)


def _proj(x, g, w, tables, chunks, seq, name):
    n, d = x.shape
    assert n % ROW_TILE == 0 and seq % ROW_TILE == 0 and w.shape[1] == d * len(chunks)
    pos_blocks = seq // ROW_TILE
    row = pl.BlockSpec((ROW_TILE, d), lambda i: (i, 0))
    tab = pl.BlockSpec((3, ROW_TILE, LANES), lambda i: (0, i % pos_blocks, 0))
    dils = [dil for _, ds in chunks for dil in ds]
    wide_rows = max([dil * _phase_pitch(ROW_TILE // dil) for dil in dils if dil > STRIDED_LOAD_MAX],
                    default=8)
    return pl.pallas_call(
        functools.partial(_proj_kernel, chunks=chunks),
        grid=(n // ROW_TILE,),
        in_specs=[row, _resident((1, d)), _resident(w.shape), tab, tab],
        out_specs=[pl.BlockSpec((ROW_TILE // dil, dil * d), lambda i: (i, 0)) for dil in dils],
        out_shape=[jax.ShapeDtypeStruct((n // dil, dil * d), BF16) for dil in dils],
        scratch_shapes=[pltpu.VMEM((d // LANES, ROW_TILE, LANES), F32),
                        pltpu.VMEM((d // LANES, wide_rows, LANES), F32)],
        compiler_params=pltpu.CompilerParams(
            dimension_semantics=("parallel",), vmem_limit_bytes=VMEM_LIMIT),
        name=name,
    )(x, g.reshape(1, d), w, *tables)


def _rope_tables(seq):
    inv = ROPE_THETA ** (-jnp.arange(0, HEAD_DIM, 2, dtype=F32) / HEAD_DIM)
    ang = jnp.arange(seq, dtype=F32)[:, None] * inv[None, :]
    cos, sin = jnp.cos(ang), jnp.sin(ang)
    zero = jnp.zeros_like(sin)
    tables = jnp.stack([jnp.concatenate([cos, cos, cos, cos], axis=1),
                        jnp.concatenate([-sin, zero, -sin, zero], axis=1),
                        jnp.concatenate([zero, sin, zero, sin], axis=1)])
    scale = HEAD_DIM ** -0.5 * math.log2(math.e)
    return tables * scale, tables


def _stack_pair(q):
    lane = lax.broadcasted_iota(jnp.int32, q.shape, 1)
    first = lane < HEAD_DIM
    zero = jnp.zeros_like(q)
    return jnp.concatenate([jnp.where(first, q, zero), jnp.where(first, zero, q)], axis=0)


def _attn_a_kernel(lam_ref, subln_ref, q_ref, k_ref, v_ref, o_ref,
                   s0_ref, s1_ref, m0_ref, m1_ref, acc0_ref, acc1_ref, vt_ref, *, lambda_init, tq):
    n_keys = k_ref.shape[1]
    n_tiles = q_ref.shape[1] // tq
    lam = lam_ref[...]
    lam_full = (jnp.exp(jnp.sum(lam[0:1] * lam[1:2], axis=-1, keepdims=True))
                - jnp.exp(jnp.sum(lam[2:3] * lam[3:4], axis=-1, keepdims=True)) + lambda_init)

    sub = 8
    n_chunks = n_keys // A_KEY_CHUNK

    bufs = ((s0_ref, m0_ref, acc0_ref), (s1_ref, m1_ref, acc1_ref))

    def step(i, parity, scores=True, numer=True, final=True):
        s_new, m_new, acc_fin = bufs[parity]
        s_old, m_old, acc_out = bufs[1 - parity]
        t_new = i if scores else None
        t_old = i - 1 if numer else None
        if final:
            saved = acc_fin[...]
            pv = saved[:LANES] / saved[LANES:LANES + 1]
            o = (pv[:, :tq] - lam_full * pv[:, tq:]).T
            o = _rms_scale(o) * subln_ref[...] * (1.0 - lambda_init)
            o_ref[0, pl.ds(pl.multiple_of((i - 2) * tq, tq), tq), :] = o.astype(BF16)
        if t_new is not None:
            q2t = _stack_pair(q_ref[0, pl.ds(pl.multiple_of(t_new * tq, tq), tq), :]).T
            top = jnp.full((sub, 2 * tq), -jnp.inf, F32)
        if t_old is not None:
            m = m_old[...]
            acc = jnp.zeros((vt_ref.shape[0], 2 * tq), F32)
        for kb in range(n_chunks):
            keys = slice(kb * A_KEY_CHUNK, (kb + 1) * A_KEY_CHUNK)
            if t_new is not None:
                st = jnp.dot(k_ref[0, keys, :], q2t, preferred_element_type=F32)
                s_new[keys, :] = st
                top = jnp.maximum(top, jnp.max(st.reshape(A_KEY_CHUNK // sub, sub, 2 * tq), axis=0))
            if t_old is not None:
                e = s_old[keys, :].reshape(A_KEY_CHUNK // sub, sub, 2 * tq) - m[None]
                pt = jnp.exp2(e).reshape(A_KEY_CHUNK, 2 * tq).astype(BF16)
                acc = acc + jnp.dot(vt_ref[:, keys], pt, preferred_element_type=F32)
        if t_new is not None:
            m_new[...] = jnp.broadcast_to(jnp.max(top, axis=0, keepdims=True), m_new.shape)
        if t_old is not None:
            acc_out[...] = acc

    vt_ref[:LANES, :] = v_ref[0].T
    vt_ref[LANES:, :] = jnp.ones((vt_ref.shape[0] - LANES, n_keys), BF16)

    step(0, 0, numer=False, final=False)
    step(1, 1, final=False)

    def steps(u, carry):
        for k in range(A_STEPS_PER_ITER):
            step(A_STEPS_PER_ITER * u + 2 + k, k % 2)
        return carry

    assert (n_tiles - 2) % A_STEPS_PER_ITER == 0 and A_STEPS_PER_ITER % 2 == 0
    lax.fori_loop(0, (n_tiles - 2) // A_STEPS_PER_ITER, steps, 0)
    step(n_tiles, 0, scores=False)
    step(n_tiles + 1, 1, scores=False, numer=False)


def _attn_a(q, k, v, lam, subln, lambda_init):
    b, s, d = q.shape
    heads = d // LANES
    tq = A_Q_TILE
    assert s % (2 * tq) == 0
    blk = pl.BlockSpec((1, s, LANES), lambda bi, hi: (bi, 0, hi))
    return pl.pallas_call(
        functools.partial(_attn_a_kernel, lambda_init=lambda_init, tq=tq),
        grid=(b, heads),
        in_specs=[_resident(lam.shape), _resident((1, LANES)), blk, blk, blk],
        out_specs=blk,
        out_shape=jax.ShapeDtypeStruct((b, s, d), BF16),
        scratch_shapes=[pltpu.VMEM((s, 2 * tq), F32), pltpu.VMEM((s, 2 * tq), F32),
                        pltpu.VMEM((8, 2 * tq), F32), pltpu.VMEM((8, 2 * tq), F32),
                        pltpu.VMEM((LANES + 16, 2 * tq), F32), pltpu.VMEM((LANES + 16, 2 * tq), F32),
                        pltpu.VMEM((LANES + 16, s), BF16)],
        compiler_params=pltpu.CompilerParams(
            dimension_semantics=("parallel", "parallel"), vmem_limit_bytes=VMEM_LIMIT),
        name="diff_attn",
    )(lam, subln.reshape(1, LANES), q, k, v)


def _attn_b_kernel(q_ref, k_ref, v_ref, o_ref, lse_ref, *, seq_len, half, blocks_per_phase):
    cw = q_ref.shape[2]
    tq = min(B_Q_TILE, seq_len)
    win = tq + 2 * half
    n_pairs = cw // LANES
    j = pl.program_id(1)
    first_pair = (j % blocks_per_phase) * n_pairs

    if blocks_per_phase > 1:
        @pl.when(j % blocks_per_phase == 0)
        def _():
            lse_ref[...] = jnp.zeros(lse_ref.shape, F32)

    lane = lax.broadcasted_iota(jnp.int32, (tq, LANES), 1)
    row = lax.broadcasted_iota(jnp.int32, (2 * tq, win), 0) % tq
    col = lax.broadcasted_iota(jnp.int32, (2 * tq, win), 1)
    ones = jnp.ones((win, LANES), BF16)

    def tile(t, carry):
        i0 = pl.multiple_of(t * tq, tq)
        w0 = pl.multiple_of(jnp.clip(i0 - half, 0, seq_len - win), half)
        band = jnp.where(jnp.abs((row + i0) - (col + w0)) <= half, 0.0, NEG_INF)
        if blocks_per_phase > 1:
            lse_tile = lse_ref[0, pl.ds(i0, tq), :]
        else:
            lse_tile = jnp.zeros((tq, LANES), F32)
        for pr in range(n_pairs):
            cols = slice(pr * LANES, (pr + 1) * LANES)
            lane_a = 2 * (first_pair + pr)
            q2 = _stack_pair(q_ref[0, pl.ds(i0, tq), cols])
            kw = k_ref[0, pl.ds(w0, win), cols]
            vw = v_ref[0, pl.ds(w0, win), cols]
            s = lax.dot_general(q2, kw, (((1,), (1,)), ((), ())),
                                preferred_element_type=F32)
            s = s + band
            m = jnp.max(s, axis=-1, keepdims=True)
            p = jnp.exp2(s - m)
            pvl = jnp.dot(p.astype(BF16), jnp.concatenate([vw, ones], axis=1),
                          preferred_element_type=F32)
            l = pvl[:, LANES:]
            pv = pvl[:, :LANES]
            o_ref[0, pl.ds(i0, tq), cols] = jnp.where(
                lane < HEAD_DIM, pv[:tq], pv[tq:]).astype(BF16)
            lse_tile = jnp.where(
                lane == lane_a, m[:tq], jnp.where(
                    lane == lane_a + 1, m[tq:], jnp.where(
                        lane == lane_a + B_SUM_LANE, l[:tq], jnp.where(
                            lane == lane_a + B_SUM_LANE + 1, l[tq:], lse_tile))))
        lse_ref[0, pl.ds(i0, tq), :] = lse_tile
        return carry

    n_tiles = seq_len // tq
    lax.fori_loop(0, n_tiles, tile, 0, unroll=min(n_tiles, max(1, B_CHAINS // n_pairs)))


def _attn_b_group(q, k, v, b, dilation, half):
    d = q.shape[1] // dilation
    assert d // HEAD_DIM <= B_SUM_LANE
    sl = q.shape[0] // b
    tq = min(B_Q_TILE, sl)
    assert sl % tq == 0 and sl >= tq + 2 * half
    cw = max(LANES, min(d, (1 << 20) // sl))
    assert d % cw == 0
    blocks_per_phase = d // cw
    view = lambda t: t.reshape(b, sl, dilation * d)
    blk = pl.BlockSpec((1, sl, cw), lambda bi, j: (bi, 0, j))
    lse_blk = pl.BlockSpec((1, sl, LANES), lambda bi, j: (bi, 0, j // blocks_per_phase))
    o, lse = pl.pallas_call(
        functools.partial(_attn_b_kernel, seq_len=sl, half=half, blocks_per_phase=blocks_per_phase),
        grid=(b, dilation * blocks_per_phase),
        in_specs=[blk, blk, blk],
        out_specs=[blk, lse_blk],
        out_shape=[jax.ShapeDtypeStruct((b, sl, dilation * d), BF16),
                   jax.ShapeDtypeStruct((b, sl, dilation * LANES), F32)],
        compiler_params=pltpu.CompilerParams(
            dimension_semantics=("parallel", "arbitrary"), vmem_limit_bytes=VMEM_LIMIT),
        name=f"dilated_attn_d{dilation}",
    )(view(q), view(k), view(v))
    return o.reshape(b * sl, dilation * d), lse.reshape(b * sl, dilation * LANES)


def _mix_out_kernel(x_ref, *rest, dilations):
    n_g = len(dilations)
    o_refs, l_refs = rest[:n_g], rest[n_g:2 * n_g]
    e_ref, w_ref, y_ref = rest[2 * n_g:2 * n_g + 3]
    stages = list(rest[2 * n_g + 3:])
    tm, d = x_ref.shape
    n_lanes = d // LANES

    lses, o_stage = [], []
    for o_ref, l_ref, dil in zip(o_refs, l_refs, dilations):
        if dil == 1:
            lses.append(l_ref[...])
            o_stage.append(None)
            continue
        ost, lst = stages.pop(0), stages.pop(0)
        for p in range(dil):
            rows = pl.ds(p, tm // dil, stride=dil)
            lst[rows, :] = l_ref[:, p * LANES:(p + 1) * LANES]
            for j in range(n_lanes):
                ost[j, rows, :] = o_ref[:, p * d + j * LANES:p * d + (j + 1) * LANES].astype(F32)
        lses.append(lst[...])
        o_stage.append(ost)

    top = functools.reduce(jnp.maximum, lses)
    ws = [jnp.exp2(t - top) for t in lses]
    sums = [pltpu.roll(t, LANES - B_SUM_LANE, 1) for t in lses]
    den = functools.reduce(jnp.add, [w * l for w, l in zip(ws, sums)])
    head_lane = lax.broadcasted_iota(jnp.int32, den.shape, 1) < B_SUM_LANE
    inv = 1.0 / jnp.where(head_lane, den, 1.0)
    e = e_ref[...]
    spreads = []
    for wgt in ws:
        alpha = wgt * inv
        hi = alpha.astype(BF16)
        lo = (alpha - hi.astype(F32)).astype(BF16)
        spreads.append(jnp.dot(jnp.concatenate([hi, lo], axis=1), e, preferred_element_type=F32))
    parts = []
    for j in range(n_lanes):
        lanes = slice(j * LANES, (j + 1) * LANES)
        acc = jnp.zeros((tm, LANES), F32)
        for o_ref, ost, spread in zip(o_refs, o_stage, spreads):
            og = o_ref[:, lanes].astype(F32) if ost is None else ost[j]
            acc = acc + spread[:, lanes] * og
        parts.append(acc.astype(BF16))
    y_ref[...] = x_ref[...] + jnp.dot(jnp.concatenate(parts, axis=1), w_ref[...].astype(BF16),
                                      preferred_element_type=F32)


def _mix_out(x, outs, lses, dilations, w):
    n, d = x.shape
    assert d // HEAD_DIM <= LANES
    spread = (jnp.arange(2 * LANES)[:, None] % LANES
              == (jnp.arange(d)[None, :] // HEAD_DIM)).astype(BF16)
    row = pl.BlockSpec((ROW_TILE, d), lambda i: (i, 0))
    o_specs = [pl.BlockSpec((ROW_TILE // dil, dil * d), lambda i: (i, 0)) for dil in dilations]
    l_specs = [pl.BlockSpec((ROW_TILE // dil, dil * LANES), lambda i: (i, 0)) for dil in dilations]
    scratch = []
    for dil in dilations:
        if dil > 1:
            scratch += [pltpu.VMEM((d // LANES, ROW_TILE, LANES), F32),
                        pltpu.VMEM((ROW_TILE, LANES), F32)]
    return pl.pallas_call(
        functools.partial(_mix_out_kernel, dilations=tuple(dilations)),
        grid=(n // ROW_TILE,),
        in_specs=[row, *o_specs, *l_specs, _resident(spread.shape), _resident(w.shape)],
        out_specs=row,
        out_shape=jax.ShapeDtypeStruct((n, d), F32),
        scratch_shapes=scratch,
        compiler_params=pltpu.CompilerParams(
            dimension_semantics=("parallel",), vmem_limit_bytes=VMEM_LIMIT),
        name="mix_out_proj",
    )(x, *outs, *lses, spread, w)


def kernel(x, ln_ffn1, w1_gate, w1_up, w1_down, ln_mix, a_w_qkv, a_w_o, a_lambda, a_subln,
           b_w_in, b_w_o, ln_ffn2, w2_gate, w2_up, w2_down, ln_final):
    b, s, d = x.shape
    depth = ln_ffn1.shape[0]
    n = b * s
    tables = _rope_tables(s)
    xf = x.reshape(n, d)
    for i in range(depth):
        xf = _ffn(xf, ln_ffn1[i], w1_gate, w1_up, w1_down, i)
        j = i // 2
        if i % 2 == 0:
            lambda_init = 0.8 - 0.6 * math.exp(-0.3 * i)
            q, k, v = _proj(xf, ln_mix[i], a_w_qkv[j], tables, (("q", (1,)), ("k", (1,)), ("v", (1,))),
                            s, "qkv_proj")
            o = _attn_a(q.reshape(b, s, d), k.reshape(b, s, d), v.reshape(b, s, d),
                        a_lambda[j], a_subln[j], lambda_init)
            mixer = (o.reshape(n, d), a_w_o[j])
        else:
            n_groups = len(B_GROUPS)
            w = b_w_in[j].astype(BF16)
            dilations = tuple(dil for _, dil in B_GROUPS)
            chunks = sum(((("q", (dil,)), ("k", (dil,))) for dil in dilations), ())
            chunks += (("v", dilations),)
            proj = _proj(xf, ln_mix[i], w, tables, chunks, s, "dilated_proj")
            outs, lses = [], []
            for g, (window, dilation) in enumerate(B_GROUPS):
                o, lse = _attn_b_group(proj[2 * g], proj[2 * g + 1], proj[2 * n_groups + g],
                                       b, dilation, window // (2 * dilation))
                outs.append(o)
                lses.append(lse)
            xf = _mix_out(xf, outs, lses, dilations, b_w_o[j])
            mixer = None
        xf = _ffn(xf, ln_ffn2[i], w2_gate, w2_up, w2_down, i, mixer=mixer,
                  final_g=ln_final if i == depth - 1 else None)
    return xf.reshape(b, s, d)
```

```python
import functools
import math

import jax
import jax.numpy as jnp
from jax import lax
from jax.experimental import pallas as pl
from jax.experimental.pallas import tpu as pltpu

EPS = 1e-6
ROPE_THETA = 10000.0
NEG_INF = -1e30
HEAD_DIM = 64
ROT_HALF = HEAD_DIM // 2
LANES = 128
B_GROUPS = ((128, 1), (512, 4), (2048, 16))

ROW_TILE = 512
FF_CHUNK = 256
A_Q_TILE = 128
A_KEY_CHUNK = 256
A_STEPS_PER_ITER = 6
B_SUM_LANE = 16
B_CHAINS = 32
B_Q_TILE = 128
VMEM_LIMIT = 56 * 1024 * 1024

F32 = jnp.float32
BF16 = jnp.bfloat16


def _resident(shape):
    return pl.BlockSpec(shape, lambda *_: (0,) * len(shape), pipeline_mode=pl.Buffered(1))


def _rms_scale(x):
    return x * lax.rsqrt(jnp.mean(x * x, axis=-1, keepdims=True) + EPS)


def _ffn_kernel(x_ref, g_ref, wg_ref, wu_ref, wd_ref, *rest, n_chunks, mixer, final):
    rest = list(rest)
    o_ref = rest.pop()
    x = x_ref[...]
    if mixer:
        a_ref, wo_ref = rest.pop(0), rest.pop(0)
        x = x + jnp.dot(a_ref[...], wo_ref[...].astype(BF16), preferred_element_type=F32)
    if final:
        gf_ref = rest.pop(0)
    h = (_rms_scale(x) * g_ref[...]).astype(BF16)
    acc = jnp.zeros(x.shape, F32)
    for c in range(n_chunks):
        cols = slice(c * FF_CHUNK, (c + 1) * FF_CHUNK)
        gate = jnp.dot(h, wg_ref[:, cols].astype(BF16), preferred_element_type=F32)
        up = jnp.dot(h, wu_ref[:, cols].astype(BF16), preferred_element_type=F32)
        a = (gate * jax.nn.sigmoid(gate) * up).astype(BF16)
        acc = acc + jnp.dot(a, wd_ref[cols, :].astype(BF16), preferred_element_type=F32)
    y = x + 0.5 * acc
    if final:
        y = _rms_scale(y) * gf_ref[...]
    o_ref[...] = y


def _layer(shape, layer):
    return pl.BlockSpec((None, *shape), lambda *_: (layer,) + (0,) * len(shape),
                        pipeline_mode=pl.Buffered(1))


def _ffn(x, g, wg, wu, wd, layer, mixer=None, final_g=None):
    n, d = x.shape
    ff = wg.shape[2]
    assert n % ROW_TILE == 0 and ff % FF_CHUNK == 0
    final = final_g is not None
    row = pl.BlockSpec((ROW_TILE, d), lambda i: (i, 0))
    in_specs = [row, _resident((1, d)), _layer((d, ff), layer), _layer((d, ff), layer),
                _layer((ff, d), layer)]
    args = [x, g.reshape(1, d), wg, wu, wd]
    if mixer is not None:
        in_specs += [row, _resident((d, d))]
        args += list(mixer)
    if final:
        in_specs.append(_resident((1, d)))
        args.append(final_g.reshape(1, d))
    return pl.pallas_call(
        functools.partial(_ffn_kernel, n_chunks=ff // FF_CHUNK, mixer=mixer is not None,
                          final=final),
        grid=(n // ROW_TILE,),
        in_specs=in_specs,
        out_specs=row,
        out_shape=jax.ShapeDtypeStruct((n, d), F32),
        compiler_params=pltpu.CompilerParams(
            dimension_semantics=("parallel",), vmem_limit_bytes=VMEM_LIMIT),
        name="ffn" + ("_mixer" if mixer is not None else "") + ("_final" if final else ""),
    )(*args)


STRIDED_LOAD_MAX = 4


def _phase_pitch(rows):
    return rows if (rows // 8) % 2 else rows + 8


def _proj_kernel(x_ref, g_ref, w_ref, tq_ref, tk_ref, *rest, chunks):
    out_refs, stage_ref, wide_ref = rest[:-2], rest[-2], rest[-1]
    x = x_ref[...]
    tm, d = x.shape
    h = (_rms_scale(x) * g_ref[...]).astype(BF16)
    o = 0
    for c, (kind, dilations) in enumerate(chunks):
        y = jnp.dot(h, w_ref[:, c * d:(c + 1) * d].astype(BF16), preferred_element_type=F32)
        if kind != "v":
            t_ref = tq_ref if kind == "q" else tk_ref
            cos, sin_up, sin_dn = t_ref[0], t_ref[1], t_ref[2]
        for j in range(d // LANES):
            lanes = slice(j * LANES, (j + 1) * LANES)
            r = y[:, lanes]
            if kind != "v":
                r = (r * cos + pltpu.roll(r, LANES - ROT_HALF, 1) * sin_up
                     + pltpu.roll(r, ROT_HALF, 1) * sin_dn)
            if any(1 < dil <= STRIDED_LOAD_MAX for dil in dilations):
                stage_ref[j] = r
            for k, dil in enumerate(dilations):
                rows = tm // dil
                if dil == 1:
                    out_refs[o + k][:, lanes] = r.astype(BF16)
                    continue
                if dil <= STRIDED_LOAD_MAX:
                    for p in range(dil):
                        out_refs[o + k][:, p * d + j * LANES:p * d + (j + 1) * LANES] = (
                            stage_ref[j, pl.ds(p, rows, stride=dil), :].astype(BF16))
                    continue
                pitch = _phase_pitch(rows)
                for m0 in range(rows):
                    wide_ref[j, pl.ds(m0, dil, stride=pitch), :] = r[m0 * dil:(m0 + 1) * dil, :]
                for p in range(dil):
                    out_refs[o + k][:, p * d + j * LANES:p * d + (j + 1) * LANES] = (
                        wide_ref[j, p * pitch:p * pitch + rows, :].astype(BF16))
        o += len(dilations)


def _proj(x, g, w, tables, chunks, seq, name):
    n, d = x.shape
    assert n % ROW_TILE == 0 and seq % ROW_TILE == 0 and w.shape[1] == d * len(chunks)
    pos_blocks = seq // ROW_TILE
    row = pl.BlockSpec((ROW_TILE, d), lambda i: (i, 0))
    tab = pl.BlockSpec((3, ROW_TILE, LANES), lambda i: (0, i % pos_blocks, 0))
    dils = [dil for _, ds in chunks for dil in ds]
    wide_rows = max([dil * _phase_pitch(ROW_TILE // dil) for dil in dils if dil > STRIDED_LOAD_MAX],
                    default=8)
    return pl.pallas_call(
        functools.partial(_proj_kernel, chunks=chunks),
        grid=(n // ROW_TILE,),
        in_specs=[row, _resident((1, d)), _resident(w.shape), tab, tab],
        out_specs=[pl.BlockSpec((ROW_TILE // dil, dil * d), lambda i: (i, 0)) for dil in dils],
        out_shape=[jax.ShapeDtypeStruct((n // dil, dil * d), BF16) for dil in dils],
        scratch_shapes=[pltpu.VMEM((d // LANES, ROW_TILE, LANES), F32),
                        pltpu.VMEM((d // LANES, wide_rows, LANES), F32)],
        compiler_params=pltpu.CompilerParams(
            dimension_semantics=("parallel",), vmem_limit_bytes=VMEM_LIMIT),
        name=name,
    )(x, g.reshape(1, d), w, *tables)


def _rope_tables(seq):
    inv = ROPE_THETA ** (-jnp.arange(0, HEAD_DIM, 2, dtype=F32) / HEAD_DIM)
    ang = jnp.arange(seq, dtype=F32)[:, None] * inv[None, :]
    cos, sin = jnp.cos(ang), jnp.sin(ang)
    zero = jnp.zeros_like(sin)
    tables = jnp.stack([jnp.concatenate([cos, cos, cos, cos], axis=1),
                        jnp.concatenate([-sin, zero, -sin, zero], axis=1),
                        jnp.concatenate([zero, sin, zero, sin], axis=1)])
    scale = HEAD_DIM ** -0.5 * math.log2(math.e)
    return tables * scale, tables


def _stack_pair(q):
    lane = lax.broadcasted_iota(jnp.int32, q.shape, 1)
    first = lane < HEAD_DIM
    zero = jnp.zeros_like(q)
    return jnp.concatenate([jnp.where(first, q, zero), jnp.where(first, zero, q)], axis=0)


def _attn_a_kernel(lam_ref, subln_ref, q_ref, k_ref, v_ref, o_ref,
                   s0_ref, s1_ref, m0_ref, m1_ref, acc0_ref, acc1_ref, vt_ref, *, lambda_init, tq):
    n_keys = k_ref.shape[1]
    n_tiles = q_ref.shape[1] // tq
    lam = lam_ref[...]
    lam_full = (jnp.exp(jnp.sum(lam[0:1] * lam[1:2], axis=-1, keepdims=True))
                - jnp.exp(jnp.sum(lam[2:3] * lam[3:4], axis=-1, keepdims=True)) + lambda_init)

    sub = 8
    n_chunks = n_keys // A_KEY_CHUNK

    bufs = ((s0_ref, m0_ref, acc0_ref), (s1_ref, m1_ref, acc1_ref))

    def step(i, parity, scores=True, numer=True, final=True):
        s_new, m_new, acc_fin = bufs[parity]
        s_old, m_old, acc_out = bufs[1 - parity]
        t_new = i if scores else None
        t_old = i - 1 if numer else None
        if final:
            saved = acc_fin[...]
            pv = saved[:LANES] / saved[LANES:LANES + 1]
            o = (pv[:, :tq] - lam_full * pv[:, tq:]).T
            o = _rms_scale(o) * subln_ref[...] * (1.0 - lambda_init)
            o_ref[0, pl.ds(pl.multiple_of((i - 2) * tq, tq), tq), :] = o.astype(BF16)
        if t_new is not None:
            q2t = _stack_pair(q_ref[0, pl.ds(pl.multiple_of(t_new * tq, tq), tq), :]).T
            top = jnp.full((sub, 2 * tq), -jnp.inf, F32)
        if t_old is not None:
            m = m_old[...]
            acc = jnp.zeros((vt_ref.shape[0], 2 * tq), F32)
        for kb in range(n_chunks):
            keys = slice(kb * A_KEY_CHUNK, (kb + 1) * A_KEY_CHUNK)
            if t_new is not None:
                st = jnp.dot(k_ref[0, keys, :], q2t, preferred_element_type=F32)
                s_new[keys, :] = st
                top = jnp.maximum(top, jnp.max(st.reshape(A_KEY_CHUNK // sub, sub, 2 * tq), axis=0))
            if t_old is not None:
                e = s_old[keys, :].reshape(A_KEY_CHUNK // sub, sub, 2 * tq) - m[None]
                pt = jnp.exp2(e).reshape(A_KEY_CHUNK, 2 * tq).astype(BF16)
                acc = acc + jnp.dot(vt_ref[:, keys], pt, preferred_element_type=F32)
        if t_new is not None:
            m_new[...] = jnp.broadcast_to(jnp.max(top, axis=0, keepdims=True), m_new.shape)
        if t_old is not None:
            acc_out[...] = acc

    vt_ref[:LANES, :] = v_ref[0].T
    vt_ref[LANES:, :] = jnp.ones((vt_ref.shape[0] - LANES, n_keys), BF16)

    step(0, 0, numer=False, final=False)
    step(1, 1, final=False)

    def steps(u, carry):
        for k in range(A_STEPS_PER_ITER):
            step(A_STEPS_PER_ITER * u + 2 + k, k % 2)
        return carry

    assert (n_tiles - 2) % A_STEPS_PER_ITER == 0 and A_STEPS_PER_ITER % 2 == 0
    lax.fori_loop(0, (n_tiles - 2) // A_STEPS_PER_ITER, steps, 0)
    step(n_tiles, 0, scores=False)
    step(n_tiles + 1, 1, scores=False, numer=False)


def _attn_a(q, k, v, lam, subln, lambda_init):
    b, s, d = q.shape
    heads = d // LANES
    tq = A_Q_TILE
    assert s % (2 * tq) == 0
    blk = pl.BlockSpec((1, s, LANES), lambda bi, hi: (bi, 0, hi))
    return pl.pallas_call(
        functools.partial(_attn_a_kernel, lambda_init=lambda_init, tq=tq),
        grid=(b, heads),
        in_specs=[_resident(lam.shape), _resident((1, LANES)), blk, blk, blk],
        out_specs=blk,
        out_shape=jax.ShapeDtypeStruct((b, s, d), BF16),
        scratch_shapes=[pltpu.VMEM((s, 2 * tq), F32), pltpu.VMEM((s, 2 * tq), F32),
                        pltpu.VMEM((8, 2 * tq), F32), pltpu.VMEM((8, 2 * tq), F32),
                        pltpu.VMEM((LANES + 16, 2 * tq), F32), pltpu.VMEM((LANES + 16, 2 * tq), F32),
                        pltpu.VMEM((LANES + 16, s), BF16)],
        compiler_params=pltpu.CompilerParams(
            dimension_semantics=("parallel", "parallel"), vmem_limit_bytes=VMEM_LIMIT),
        name="diff_attn",
    )(lam, subln.reshape(1, LANES), q, k, v)


def _attn_b_kernel(q_ref, k_ref, v_ref, o_ref, lse_ref, *, seq_len, half, blocks_per_phase):
    cw = q_ref.shape[2]
    tq = min(B_Q_TILE, seq_len)
    win = tq + 2 * half
    n_pairs = cw // LANES
    j = pl.program_id(1)
    first_pair = (j % blocks_per_phase) * n_pairs

    if blocks_per_phase > 1:
        @pl.when(j % blocks_per_phase == 0)
        def _():
            lse_ref[...] = jnp.zeros(lse_ref.shape, F32)

    lane = lax.broadcasted_iota(jnp.int32, (tq, LANES), 1)
    row = lax.broadcasted_iota(jnp.int32, (2 * tq, win), 0) % tq
    col = lax.broadcasted_iota(jnp.int32, (2 * tq, win), 1)
    ones = jnp.ones((win, LANES), BF16)

    def tile(t, carry):
        i0 = pl.multiple_of(t * tq, tq)
        w0 = pl.multiple_of(jnp.clip(i0 - half, 0, seq_len - win), half)
        band = jnp.where(jnp.abs((row + i0) - (col + w0)) <= half, 0.0, NEG_INF)
        if blocks_per_phase > 1:
            lse_tile = lse_ref[0, pl.ds(i0, tq), :]
        else:
            lse_tile = jnp.zeros((tq, LANES), F32)
        for pr in range(n_pairs):
            cols = slice(pr * LANES, (pr + 1) * LANES)
            lane_a = 2 * (first_pair + pr)
            q2 = _stack_pair(q_ref[0, pl.ds(i0, tq), cols])
            kw = k_ref[0, pl.ds(w0, win), cols]
            vw = v_ref[0, pl.ds(w0, win), cols]
            s = lax.dot_general(q2, kw, (((1,), (1,)), ((), ())),
                                preferred_element_type=F32)
            s = s + band
            m = jnp.max(s, axis=-1, keepdims=True)
            p = jnp.exp2(s - m)
            pvl = jnp.dot(p.astype(BF16), jnp.concatenate([vw, ones], axis=1),
                          preferred_element_type=F32)
            l = pvl[:, LANES:]
            pv = pvl[:, :LANES]
            o_ref[0, pl.ds(i0, tq), cols] = jnp.where(
                lane < HEAD_DIM, pv[:tq], pv[tq:]).astype(BF16)
            lse_tile = jnp.where(
                lane == lane_a, m[:tq], jnp.where(
                    lane == lane_a + 1, m[tq:], jnp.where(
                        lane == lane_a + B_SUM_LANE, l[:tq], jnp.where(
                            lane == lane_a + B_SUM_LANE + 1, l[tq:], lse_tile))))
        lse_ref[0, pl.ds(i0, tq), :] = lse_tile
        return carry

    n_tiles = seq_len // tq
    lax.fori_loop(0, n_tiles, tile, 0, unroll=min(n_tiles, max(1, B_CHAINS // n_pairs)))


def _attn_b_group(q, k, v, b, dilation, half):
    d = q.shape[1] // dilation
    assert d // HEAD_DIM <= B_SUM_LANE
    sl = q.shape[0] // b
    tq = min(B_Q_TILE, sl)
    assert sl % tq == 0 and sl >= tq + 2 * half
    cw = max(LANES, min(d, (1 << 20) // sl))
    assert d % cw == 0
    blocks_per_phase = d // cw
    view = lambda t: t.reshape(b, sl, dilation * d)
    blk = pl.BlockSpec((1, sl, cw), lambda bi, j: (bi, 0, j))
    lse_blk = pl.BlockSpec((1, sl, LANES), lambda bi, j: (bi, 0, j // blocks_per_phase))
    o, lse = pl.pallas_call(
        functools.partial(_attn_b_kernel, seq_len=sl, half=half, blocks_per_phase=blocks_per_phase),
        grid=(b, dilation * blocks_per_phase),
        in_specs=[blk, blk, blk],
        out_specs=[blk, lse_blk],
        out_shape=[jax.ShapeDtypeStruct((b, sl, dilation * d), BF16),
                   jax.ShapeDtypeStruct((b, sl, dilation * LANES), F32)],
        compiler_params=pltpu.CompilerParams(
            dimension_semantics=("parallel", "arbitrary"), vmem_limit_bytes=VMEM_LIMIT),
        name=f"dilated_attn_d{dilation}",
    )(view(q), view(k), view(v))
    return o.reshape(b * sl, dilation * d), lse.reshape(b * sl, dilation * LANES)


def _mix_out_kernel(x_ref, *rest, dilations):
    n_g = len(dilations)
    o_refs, l_refs = rest[:n_g], rest[n_g:2 * n_g]
    e_ref, w_ref, y_ref = rest[2 * n_g:2 * n_g + 3]
    stages = list(rest[2 * n_g + 3:])
    tm, d = x_ref.shape
    n_lanes = d // LANES

    lses, o_stage = [], []
    for o_ref, l_ref, dil in zip(o_refs, l_refs, dilations):
        if dil == 1:
            lses.append(l_ref[...])
            o_stage.append(None)
            continue
        ost, lst = stages.pop(0), stages.pop(0)
        for p in range(dil):
            rows = pl.ds(p, tm // dil, stride=dil)
            lst[rows, :] = l_ref[:, p * LANES:(p + 1) * LANES]
            for j in range(n_lanes):
                ost[j, rows, :] = o_ref[:, p * d + j * LANES:p * d + (j + 1) * LANES].astype(F32)
        lses.append(lst[...])
        o_stage.append(ost)

    top = functools.reduce(jnp.maximum, lses)
    ws = [jnp.exp2(t - top) for t in lses]
    sums = [pltpu.roll(t, LANES - B_SUM_LANE, 1) for t in lses]
    den = functools.reduce(jnp.add, [w * l for w, l in zip(ws, sums)])
    head_lane = lax.broadcasted_iota(jnp.int32, den.shape, 1) < B_SUM_LANE
    inv = 1.0 / jnp.where(head_lane, den, 1.0)
    e = e_ref[...]
    spreads = []
    for wgt in ws:
        alpha = wgt * inv
        hi = alpha.astype(BF16)
        lo = (alpha - hi.astype(F32)).astype(BF16)
        spreads.append(jnp.dot(jnp.concatenate([hi, lo], axis=1), e, preferred_element_type=F32))
    parts = []
    for j in range(n_lanes):
        lanes = slice(j * LANES, (j + 1) * LANES)
        acc = jnp.zeros((tm, LANES), F32)
        for o_ref, ost, spread in zip(o_refs, o_stage, spreads):
            og = o_ref[:, lanes].astype(F32) if ost is None else ost[j]
            acc = acc + spread[:, lanes] * og
        parts.append(acc.astype(BF16))
    y_ref[...] = x_ref[...] + jnp.dot(jnp.concatenate(parts, axis=1), w_ref[...].astype(BF16),
                                      preferred_element_type=F32)


def _mix_out(x, outs, lses, dilations, w):
    n, d = x.shape
    assert d // HEAD_DIM <= LANES
    spread = (jnp.arange(2 * LANES)[:, None] % LANES
              == (jnp.arange(d)[None, :] // HEAD_DIM)).astype(BF16)
    row = pl.BlockSpec((ROW_TILE, d), lambda i: (i, 0))
    o_specs = [pl.BlockSpec((ROW_TILE // dil, dil * d), lambda i: (i, 0)) for dil in dilations]
    l_specs = [pl.BlockSpec((ROW_TILE // dil, dil * LANES), lambda i: (i, 0)) for dil in dilations]
    scratch = []
    for dil in dilations:
        if dil > 1:
            scratch += [pltpu.VMEM((d // LANES, ROW_TILE, LANES), F32),
                        pltpu.VMEM((ROW_TILE, LANES), F32)]
    return pl.pallas_call(
        functools.partial(_mix_out_kernel, dilations=tuple(dilations)),
        grid=(n // ROW_TILE,),
        in_specs=[row, *o_specs, *l_specs, _resident(spread.shape), _resident(w.shape)],
        out_specs=row,
        out_shape=jax.ShapeDtypeStruct((n, d), F32),
        scratch_shapes=scratch,
        compiler_params=pltpu.CompilerParams(
            dimension_semantics=("parallel",), vmem_limit_bytes=VMEM_LIMIT),
        name="mix_out_proj",
    )(x, *outs, *lses, spread, w)


def kernel(x, ln_ffn1, w1_gate, w1_up, w1_down, ln_mix, a_w_qkv, a_w_o, a_lambda, a_subln,
           b_w_in, b_w_o, ln_ffn2, w2_gate, w2_up, w2_down, ln_final):
    b, s, d = x.shape
    depth = ln_ffn1.shape[0]
    n = b * s
    tables = _rope_tables(s)
    xf = x.reshape(n, d)
    for i in range(depth):
        xf = _ffn(xf, ln_ffn1[i], w1_gate, w1_up, w1_down, i)
        j = i // 2
        if i % 2 == 0:
            lambda_init = 0.8 - 0.6 * math.exp(-0.3 * i)
            q, k, v = _proj(xf, ln_mix[i], a_w_qkv[j], tables, (("q", (1,)), ("k", (1,)), ("v", (1,))),
                            s, "qkv_proj")
            o = _attn_a(q.reshape(b, s, d), k.reshape(b, s, d), v.reshape(b, s, d),
                        a_lambda[j], a_subln[j], lambda_init)
            mixer = (o.reshape(n, d), a_w_o[j])
        else:
            n_groups = len(B_GROUPS)
            w = b_w_in[j].astype(BF16)
            dilations = tuple(dil for _, dil in B_GROUPS)
            chunks = sum(((("q", (dil,)), ("k", (dil,))) for dil in dilations), ())
            chunks += (("v", dilations),)
            proj = _proj(xf, ln_mix[i], w, tables, chunks, s, "dilated_proj")
            outs, lses = [], []
            for g, (window, dilation) in enumerate(B_GROUPS):
                o, lse = _attn_b_group(proj[2 * g], proj[2 * g + 1], proj[2 * n_groups + g],
                                       b, dilation, window // (2 * dilation))
                outs.append(o)
                lses.append(lse)
            xf = _mix_out(xf, outs, lses, dilations, b_w_o[j])
            mixer = None
        xf = _ffn(xf, ln_ffn2[i], w2_gate, w2_up, w2_down, i, mixer=mixer,
                  final_g=ln_final if i == depth - 1 else None)
    return xf.reshape(b, s, d)
```

```python
import functools
import math

import jax
import jax.numpy as jnp
from jax import lax
from jax.experimental import pallas as pl
from jax.experimental.pallas import tpu as pltpu

EPS = 1e-6
ROPE_THETA = 10000.0
NEG_INF = -1e30
HEAD_DIM = 64
ROT_HALF = HEAD_DIM // 2
LANES = 128
B_GROUPS = ((128, 1), (512, 4), (2048, 16))

ROW_TILE = 512
FF_CHUNK = 256
A_Q_TILE = 256
A_KEY_CHUNK = 256
A_STEPS_PER_ITER = 2
B_SUM_LANE = 16
B_CHAINS = 32
B_Q_TILE = 128
VMEM_LIMIT = 56 * 1024 * 1024

F32 = jnp.float32
BF16 = jnp.bfloat16


def _resident(shape):
    return pl.BlockSpec(shape, lambda *_: (0,) * len(shape), pipeline_mode=pl.Buffered(1))


def _rms_scale(x):
    return x * lax.rsqrt(jnp.mean(x * x, axis=-1, keepdims=True) + EPS)


def _ffn_kernel(x_ref, g_ref, wg_ref, wu_ref, wd_ref, *rest, n_chunks, mixer, final):
    rest = list(rest)
    o_ref = rest.pop()
    x = x_ref[...]
    if mixer:
        a_ref, wo_ref = rest.pop(0), rest.pop(0)
        x = x + jnp.dot(a_ref[...], wo_ref[...].astype(BF16), preferred_element_type=F32)
    if final:
        gf_ref = rest.pop(0)
    h = (_rms_scale(x) * g_ref[...]).astype(BF16)
    acc = jnp.zeros(x.shape, F32)
    for c in range(n_chunks):
        cols = slice(c * FF_CHUNK, (c + 1) * FF_CHUNK)
        gate = jnp.dot(h, wg_ref[:, cols].astype(BF16), preferred_element_type=F32)
        up = jnp.dot(h, wu_ref[:, cols].astype(BF16), preferred_element_type=F32)
        a = (gate * jax.nn.sigmoid(gate) * up).astype(BF16)
        acc = acc + jnp.dot(a, wd_ref[cols, :].astype(BF16), preferred_element_type=F32)
    y = x + 0.5 * acc
    if final:
        y = _rms_scale(y) * gf_ref[...]
    o_ref[...] = y


def _layer(shape, layer):
    return pl.BlockSpec((None, *shape), lambda *_: (layer,) + (0,) * len(shape),
                        pipeline_mode=pl.Buffered(1))


def _ffn(x, g, wg, wu, wd, layer, mixer=None, final_g=None):
    n, d = x.shape
    ff = wg.shape[2]
    assert n % ROW_TILE == 0 and ff % FF_CHUNK == 0
    final = final_g is not None
    row = pl.BlockSpec((ROW_TILE, d), lambda i: (i, 0))
    in_specs = [row, _resident((1, d)), _layer((d, ff), layer), _layer((d, ff), layer),
                _layer((ff, d), layer)]
    args = [x, g.reshape(1, d), wg, wu, wd]
    if mixer is not None:
        in_specs += [row, _resident((d, d))]
        args += list(mixer)
    if final:
        in_specs.append(_resident((1, d)))
        args.append(final_g.reshape(1, d))
    return pl.pallas_call(
        functools.partial(_ffn_kernel, n_chunks=ff // FF_CHUNK, mixer=mixer is not None,
                          final=final),
        grid=(n // ROW_TILE,),
        in_specs=in_specs,
        out_specs=row,
        out_shape=jax.ShapeDtypeStruct((n, d), F32),
        compiler_params=pltpu.CompilerParams(
            dimension_semantics=("parallel",), vmem_limit_bytes=VMEM_LIMIT),
        name="ffn" + ("_mixer" if mixer is not None else "") + ("_final" if final else ""),
    )(*args)


STRIDED_LOAD_MAX = 4


def _phase_pitch(rows):
    return rows if (rows // 8) % 2 else rows + 8


def _proj_kernel(x_ref, g_ref, w_ref, tq_ref, tk_ref, *rest, chunks):
    out_refs, stage_ref, wide_ref = rest[:-2], rest[-2], rest[-1]
    x = x_ref[...]
    tm, d = x.shape
    h = (_rms_scale(x) * g_ref[...]).astype(BF16)
    o = 0
    for c, (kind, dilations) in enumerate(chunks):
        y = jnp.dot(h, w_ref[:, c * d:(c + 1) * d].astype(BF16), preferred_element_type=F32)
        if kind != "v":
            t_ref = tq_ref if kind == "q" else tk_ref
            cos, sin_up, sin_dn = t_ref[0], t_ref[1], t_ref[2]
        for j in range(d // LANES):
            lanes = slice(j * LANES, (j + 1) * LANES)
            r = y[:, lanes]
            if kind != "v":
                r = (r * cos + pltpu.roll(r, LANES - ROT_HALF, 1) * sin_up
                     + pltpu.roll(r, ROT_HALF, 1) * sin_dn)
            if any(1 < dil <= STRIDED_LOAD_MAX for dil in dilations):
                stage_ref[j] = r
            for k, dil in enumerate(dilations):
                rows = tm // dil
                if dil == 1:
                    out_refs[o + k][:, lanes] = r.astype(BF16)
                    continue
                if dil <= STRIDED_LOAD_MAX:
                    for p in range(dil):
                        out_refs[o + k][:, p * d + j * LANES:p * d + (j + 1) * LANES] = (
                            stage_ref[j, pl.ds(p, rows, stride=dil), :].astype(BF16))
                    continue
                pitch = _phase_pitch(rows)
                for m0 in range(rows):
                    wide_ref[j, pl.ds(m0, dil, stride=pitch), :] = r[m0 * dil:(m0 + 1) * dil, :]
                for p in range(dil):
                    out_refs[o + k][:, p * d + j * LANES:p * d + (j + 1) * LANES] = (
                        wide_ref[j, p * pitch:p * pitch + rows, :].astype(BF16))
        o += len(dilations)


def _proj(x, g, w, tables, chunks, seq, name):
    n, d = x.shape
    assert n % ROW_TILE == 0 and seq % ROW_TILE == 0 and w.shape[1] == d * len(chunks)
    pos_blocks = seq // ROW_TILE
    row = pl.BlockSpec((ROW_TILE, d), lambda i: (i, 0))
    tab = pl.BlockSpec((3, ROW_TILE, LANES), lambda i: (0, i % pos_blocks, 0))
    dils = [dil for _, ds in chunks for dil in ds]
    wide_rows = max([dil * _phase_pitch(ROW_TILE // dil) for dil in dils if dil > STRIDED_LOAD_MAX],
                    default=8)
    return pl.pallas_call(
        functools.partial(_proj_kernel, chunks=chunks),
        grid=(n // ROW_TILE,),
        in_specs=[row, _resident((1, d)), _resident(w.shape), tab, tab],
        out_specs=[pl.BlockSpec((ROW_TILE // dil, dil * d), lambda i: (i, 0)) for dil in dils],
        out_shape=[jax.ShapeDtypeStruct((n // dil, dil * d), BF16) for dil in dils],
        scratch_shapes=[pltpu.VMEM((d // LANES, ROW_TILE, LANES), F32),
                        pltpu.VMEM((d // LANES, wide_rows, LANES), F32)],
        compiler_params=pltpu.CompilerParams(
            dimension_semantics=("parallel",), vmem_limit_bytes=VMEM_LIMIT),
        name=name,
    )(x, g.reshape(1, d), w, *tables)


def _rope_tables(seq):
    inv = ROPE_THETA ** (-jnp.arange(0, HEAD_DIM, 2, dtype=F32) / HEAD_DIM)
    ang = jnp.arange(seq, dtype=F32)[:, None] * inv[None, :]
    cos, sin = jnp.cos(ang), jnp.sin(ang)
    zero = jnp.zeros_like(sin)
    tables = jnp.stack([jnp.concatenate([cos, cos, cos, cos], axis=1),
                        jnp.concatenate([-sin, zero, -sin, zero], axis=1),
                        jnp.concatenate([zero, sin, zero, sin], axis=1)])
    scale = HEAD_DIM ** -0.5 * math.log2(math.e)
    return tables * scale, tables


def _stack_pair(q):
    lane = lax.broadcasted_iota(jnp.int32, q.shape, 1)
    first = lane < HEAD_DIM
    zero = jnp.zeros_like(q)
    return jnp.concatenate([jnp.where(first, q, zero), jnp.where(first, zero, q)], axis=0)


def _attn_a_kernel(lam_ref, subln_ref, q_ref, k_ref, v_ref, o_ref,
                   s0_ref, s1_ref, m0_ref, m1_ref, acc0_ref, acc1_ref, vt_ref, *, lambda_init, tq):
    n_keys = k_ref.shape[1]
    n_tiles = q_ref.shape[1] // tq
    lam = lam_ref[...]
    lam_full = (jnp.exp(jnp.sum(lam[0:1] * lam[1:2], axis=-1, keepdims=True))
                - jnp.exp(jnp.sum(lam[2:3] * lam[3:4], axis=-1, keepdims=True)) + lambda_init)

    sub = 8
    n_chunks = n_keys // A_KEY_CHUNK

    bufs = ((s0_ref, m0_ref, acc0_ref), (s1_ref, m1_ref, acc1_ref))

    def step(i, parity, scores=True, numer=True, final=True):
        s_new, m_new, acc_fin = bufs[parity]
        s_old, m_old, acc_out = bufs[1 - parity]
        t_new = i if scores else None
        t_old = i - 1 if numer else None
        if final:
            saved = acc_fin[...]
            pv = saved[:LANES] / saved[LANES:LANES + 1]
            o = (pv[:, :tq] - lam_full * pv[:, tq:]).T
            o = _rms_scale(o) * subln_ref[...] * (1.0 - lambda_init)
            o_ref[0, pl.ds(pl.multiple_of((i - 2) * tq, tq), tq), :] = o.astype(BF16)
        if t_new is not None:
            q2t = _stack_pair(q_ref[0, pl.ds(pl.multiple_of(t_new * tq, tq), tq), :]).T
            top = jnp.full((sub, 2 * tq), -jnp.inf, F32)
        if t_old is not None:
            m = m_old[...]
            acc = jnp.zeros((vt_ref.shape[0], 2 * tq), F32)
        for kb in range(n_chunks):
            keys = slice(kb * A_KEY_CHUNK, (kb + 1) * A_KEY_CHUNK)
            if t_new is not None:
                st = jnp.dot(k_ref[0, keys, :], q2t, preferred_element_type=F32)
                s_new[keys, :] = st
                top = jnp.maximum(top, jnp.max(st.reshape(A_KEY_CHUNK // sub, sub, 2 * tq), axis=0))
            if t_old is not None:
                e = s_old[keys, :].reshape(A_KEY_CHUNK // sub, sub, 2 * tq) - m[None]
                pt = jnp.exp2(e).reshape(A_KEY_CHUNK, 2 * tq).astype(BF16)
                acc = acc + jnp.dot(vt_ref[:, keys], pt, preferred_element_type=F32)
        if t_new is not None:
            m_new[...] = jnp.broadcast_to(jnp.max(top, axis=0, keepdims=True), m_new.shape)
        if t_old is not None:
            acc_out[...] = acc

    vt_ref[:LANES, :] = v_ref[0].T
    vt_ref[LANES:, :] = jnp.ones((vt_ref.shape[0] - LANES, n_keys), BF16)

    step(0, 0, numer=False, final=False)
    step(1, 1, final=False)

    def steps(u, carry):
        for k in range(A_STEPS_PER_ITER):
            step(A_STEPS_PER_ITER * u + 2 + k, k % 2)
        return carry

    assert (n_tiles - 2) % A_STEPS_PER_ITER == 0 and A_STEPS_PER_ITER % 2 == 0
    lax.fori_loop(0, (n_tiles - 2) // A_STEPS_PER_ITER, steps, 0)
    step(n_tiles, 0, scores=False)
    step(n_tiles + 1, 1, scores=False, numer=False)


def _attn_a(q, k, v, lam, subln, lambda_init):
    b, s, d = q.shape
    heads = d // LANES
    tq = A_Q_TILE
    assert s % (2 * tq) == 0
    blk = pl.BlockSpec((1, s, LANES), lambda bi, hi: (bi, 0, hi))
    return pl.pallas_call(
        functools.partial(_attn_a_kernel, lambda_init=lambda_init, tq=tq),
        grid=(b, heads),
        in_specs=[_resident(lam.shape), _resident((1, LANES)), blk, blk, blk],
        out_specs=blk,
        out_shape=jax.ShapeDtypeStruct((b, s, d), BF16),
        scratch_shapes=[pltpu.VMEM((s, 2 * tq), F32), pltpu.VMEM((s, 2 * tq), F32),
                        pltpu.VMEM((8, 2 * tq), F32), pltpu.VMEM((8, 2 * tq), F32),
                        pltpu.VMEM((LANES + 16, 2 * tq), F32), pltpu.VMEM((LANES + 16, 2 * tq), F32),
                        pltpu.VMEM((LANES + 16, s), BF16)],
        compiler_params=pltpu.CompilerParams(
            dimension_semantics=("parallel", "parallel"), vmem_limit_bytes=VMEM_LIMIT),
        name="diff_attn",
    )(lam, subln.reshape(1, LANES), q, k, v)


def _attn_b_kernel(q_ref, k_ref, v_ref, o_ref, lse_ref, *, seq_len, half, blocks_per_phase):
    cw = q_ref.shape[2]
    tq = min(B_Q_TILE, seq_len)
    win = tq + 2 * half
    n_pairs = cw // LANES
    j = pl.program_id(1)
    first_pair = (j % blocks_per_phase) * n_pairs

    if blocks_per_phase > 1:
        @pl.when(j % blocks_per_phase == 0)
        def _():
            lse_ref[...] = jnp.zeros(lse_ref.shape, F32)

    lane = lax.broadcasted_iota(jnp.int32, (tq, LANES), 1)
    row = lax.broadcasted_iota(jnp.int32, (2 * tq, win), 0) % tq
    col = lax.broadcasted_iota(jnp.int32, (2 * tq, win), 1)
    ones = jnp.ones((win, LANES), BF16)

    def tile(t, carry):
        i0 = pl.multiple_of(t * tq, tq)
        w0 = pl.multiple_of(jnp.clip(i0 - half, 0, seq_len - win), half)
        band = jnp.where(jnp.abs((row + i0) - (col + w0)) <= half, 0.0, NEG_INF)
        if blocks_per_phase > 1:
            lse_tile = lse_ref[0, pl.ds(i0, tq), :]
        else:
            lse_tile = jnp.zeros((tq, LANES), F32)
        for pr in range(n_pairs):
            cols = slice(pr * LANES, (pr + 1) * LANES)
            lane_a = 2 * (first_pair + pr)
            q2 = _stack_pair(q_ref[0, pl.ds(i0, tq), cols])
            kw = k_ref[0, pl.ds(w0, win), cols]
            vw = v_ref[0, pl.ds(w0, win), cols]
            s = lax.dot_general(q2, kw, (((1,), (1,)), ((), ())),
                                preferred_element_type=F32)
            s = s + band
            m = jnp.max(s, axis=-1, keepdims=True)
            p = jnp.exp2(s - m)
            pvl = jnp.dot(p.astype(BF16), jnp.concatenate([vw, ones], axis=1),
                          preferred_element_type=F32)
            l = pvl[:, LANES:]
            pv = pvl[:, :LANES]
            o_ref[0, pl.ds(i0, tq), cols] = jnp.where(
                lane < HEAD_DIM, pv[:tq], pv[tq:]).astype(BF16)
            lse_tile = jnp.where(
                lane == lane_a, m[:tq], jnp.where(
                    lane == lane_a + 1, m[tq:], jnp.where(
                        lane == lane_a + B_SUM_LANE, l[:tq], jnp.where(
                            lane == lane_a + B_SUM_LANE + 1, l[tq:], lse_tile))))
        lse_ref[0, pl.ds(i0, tq), :] = lse_tile
        return carry

    n_tiles = seq_len // tq
    lax.fori_loop(0, n_tiles, tile, 0, unroll=min(n_tiles, max(1, B_CHAINS // n_pairs)))


def _attn_b_group(q, k, v, b, dilation, half):
    d = q.shape[1] // dilation
    assert d // HEAD_DIM <= B_SUM_LANE
    sl = q.shape[0] // b
    tq = min(B_Q_TILE, sl)
    assert sl % tq == 0 and sl >= tq + 2 * half
    cw = max(LANES, min(d, (1 << 20) // sl))
    assert d % cw == 0
    blocks_per_phase = d // cw
    view = lambda t: t.reshape(b, sl, dilation * d)
    blk = pl.BlockSpec((1, sl, cw), lambda bi, j: (bi, 0, j))
    lse_blk = pl.BlockSpec((1, sl, LANES), lambda bi, j: (bi, 0, j // blocks_per_phase))
    o, lse = pl.pallas_call(
        functools.partial(_attn_b_kernel, seq_len=sl, half=half, blocks_per_phase=blocks_per_phase),
        grid=(b, dilation * blocks_per_phase),
        in_specs=[blk, blk, blk],
        out_specs=[blk, lse_blk],
        out_shape=[jax.ShapeDtypeStruct((b, sl, dilation * d), BF16),
                   jax.ShapeDtypeStruct((b, sl, dilation * LANES), F32)],
        compiler_params=pltpu.CompilerParams(
            dimension_semantics=("parallel", "arbitrary"), vmem_limit_bytes=VMEM_LIMIT),
        name=f"dilated_attn_d{dilation}",
    )(view(q), view(k), view(v))
    return o.reshape(b * sl, dilation * d), lse.reshape(b * sl, dilation * LANES)


def _mix_out_kernel(x_ref, *rest, dilations):
    n_g = len(dilations)
    o_refs, l_refs = rest[:n_g], rest[n_g:2 * n_g]
    e_ref, w_ref, y_ref = rest[2 * n_g:2 * n_g + 3]
    stages = list(rest[2 * n_g + 3:])
    tm, d = x_ref.shape
    n_lanes = d // LANES

    lses, o_stage = [], []
    for o_ref, l_ref, dil in zip(o_refs, l_refs, dilations):
        if dil == 1:
            lses.append(l_ref[...])
            o_stage.append(None)
            continue
        ost, lst = stages.pop(0), stages.pop(0)
        for p in range(dil):
            rows = pl.ds(p, tm // dil, stride=dil)
            lst[rows, :] = l_ref[:, p * LANES:(p + 1) * LANES]
            for j in range(n_lanes):
                ost[j, rows, :] = o_ref[:, p * d + j * LANES:p * d + (j + 1) * LANES].astype(F32)
        lses.append(lst[...])
        o_stage.append(ost)

    top = functools.reduce(jnp.maximum, lses)
    ws = [jnp.exp2(t - top) for t in lses]
    sums = [pltpu.roll(t, LANES - B_SUM_LANE, 1) for t in lses]
    den = functools.reduce(jnp.add, [w * l for w, l in zip(ws, sums)])
    head_lane = lax.broadcasted_iota(jnp.int32, den.shape, 1) < B_SUM_LANE
    inv = 1.0 / jnp.where(head_lane, den, 1.0)
    e = e_ref[...]
    spreads = []
    for wgt in ws:
        alpha = wgt * inv
        hi = alpha.astype(BF16)
        lo = (alpha - hi.astype(F32)).astype(BF16)
        spreads.append(jnp.dot(jnp.concatenate([hi, lo], axis=1), e, preferred_element_type=F32))
    parts = []
    for j in range(n_lanes):
        lanes = slice(j * LANES, (j + 1) * LANES)
        acc = jnp.zeros((tm, LANES), F32)
        for o_ref, ost, spread in zip(o_refs, o_stage, spreads):
            og = o_ref[:, lanes].astype(F32) if ost is None else ost[j]
            acc = acc + spread[:, lanes] * og
        parts.append(acc.astype(BF16))
    y_ref[...] = x_ref[...] + jnp.dot(jnp.concatenate(parts, axis=1), w_ref[...].astype(BF16),
                                      preferred_element_type=F32)


def _mix_out(x, outs, lses, dilations, w):
    n, d = x.shape
    assert d // HEAD_DIM <= LANES
    spread = (jnp.arange(2 * LANES)[:, None] % LANES
              == (jnp.arange(d)[None, :] // HEAD_DIM)).astype(BF16)
    row = pl.BlockSpec((ROW_TILE, d), lambda i: (i, 0))
    o_specs = [pl.BlockSpec((ROW_TILE // dil, dil * d), lambda i: (i, 0)) for dil in dilations]
    l_specs = [pl.BlockSpec((ROW_TILE // dil, dil * LANES), lambda i: (i, 0)) for dil in dilations]
    scratch = []
    for dil in dilations:
        if dil > 1:
            scratch += [pltpu.VMEM((d // LANES, ROW_TILE, LANES), F32),
                        pltpu.VMEM((ROW_TILE, LANES), F32)]
    return pl.pallas_call(
        functools.partial(_mix_out_kernel, dilations=tuple(dilations)),
        grid=(n // ROW_TILE,),
        in_specs=[row, *o_specs, *l_specs, _resident(spread.shape), _resident(w.shape)],
        out_specs=row,
        out_shape=jax.ShapeDtypeStruct((n, d), F32),
        scratch_shapes=scratch,
        compiler_params=pltpu.CompilerParams(
            dimension_semantics=("parallel",), vmem_limit_bytes=VMEM_LIMIT),
        name="mix_out_proj",
    )(x, *outs, *lses, spread, w)


def kernel(x, ln_ffn1, w1_gate, w1_up, w1_down, ln_mix, a_w_qkv, a_w_o, a_lambda, a_subln,
           b_w_in, b_w_o, ln_ffn2, w2_gate, w2_up, w2_down, ln_final):
    b, s, d = x.shape
    depth = ln_ffn1.shape[0]
    n = b * s
    tables = _rope_tables(s)
    xf = x.reshape(n, d)
    for i in range(depth):
        xf = _ffn(xf, ln_ffn1[i], w1_gate, w1_up, w1_down, i)
        j = i // 2
        if i % 2 == 0:
            lambda_init = 0.8 - 0.6 * math.exp(-0.3 * i)
            q, k, v = _proj(xf, ln_mix[i], a_w_qkv[j], tables, (("q", (1,)), ("k", (1,)), ("v", (1,))),
                            s, "qkv_proj")
            o = _attn_a(q.reshape(b, s, d), k.reshape(b, s, d), v.reshape(b, s, d),
                        a_lambda[j], a_subln[j], lambda_init)
            mixer = (o.reshape(n, d), a_w_o[j])
        else:
            n_groups = len(B_GROUPS)
            w = b_w_in[j].astype(BF16)
            dilations = tuple(dil for _, dil in B_GROUPS)
            chunks = sum(((("q", (dil,)), ("k", (dil,))) for dil in dilations), ())
            chunks += (("v", dilations),)
            proj = _proj(xf, ln_mix[i], w, tables, chunks, s, "dilated_proj")
            outs, lses = [], []
            for g, (window, dilation) in enumerate(B_GROUPS):
                o, lse = _attn_b_group(proj[2 * g], proj[2 * g + 1], proj[2 * n_groups + g],
                                       b, dilation, window // (2 * dilation))
                outs.append(o)
                lses.append(lse)
            xf = _mix_out(xf, outs, lses, dilations, b_w_o[j])
            mixer = None
        xf = _ffn(xf, ln_ffn2[i], w2_gate, w2_up, w2_down, i, mixer=mixer,
                  final_g=ln_final if i == depth - 1 else None)
    return xf.reshape(b, s, d)
```
